```python
import jax
import jax.numpy as jnp
from jax import lax
import numpy as np

D_MODEL = 4096
BATCH = 4
SEQ = 2048
DEPTH = 4
DEC_BATCH = 128
DEC_SEQ = 1
PAST_LEN = 16384
PAGE_SIZE = 128

MIX_W = D_MODEL
POOL_W = D_MODEL // 4
POOL_WINDOWS = (2, 4, 8, 16)
N_POOL_GROUPS = len(POOL_WINDOWS)
POOL_GW = POOL_W // N_POOL_GROUPS
POOL_BUF = max(POOL_WINDOWS) - 1
SCONV_CH = (MIX_W - POOL_W) // 2
SCONV_K = 3
CCONV_CH = MIX_W - POOL_W - SCONV_CH
CCONV_K = 31
IN_W = POOL_W + 3 * SCONV_CH + 2 * CCONV_CH
IN_SPLITS = (POOL_W, POOL_W + SCONV_CH, POOL_W + 2 * SCONV_CH, POOL_W + 3 * SCONV_CH,
             POOL_W + 3 * SCONV_CH + CCONV_CH)
N_MEM = 256
X_HEADS = 4
X_HEAD_DIM = D_MODEL // X_HEADS
D_FF = ((8 * D_MODEL // 3 + 255) // 256) * 256
N_EXPERTS = 8
TOP_K = 2
D_FF_EXPERT = 2 * D_MODEL
N_DENSE = (DEPTH + 1) // 2
N_MOE = DEPTH // 2
EPS = 1e-6

kernel_name = "hybrid_pool_shortconv_conformer_xattn_moe_step"


def rmsnorm(x, g):
    xf = x.astype(jnp.float32)
    y = xf * lax.rsqrt(jnp.mean(xf * xf, axis=-1, keepdims=True) + EPS)
    return (y * g.astype(jnp.float32)).astype(x.dtype)


def layernorm(x, g, b):
    xf = x.astype(jnp.float32)
    mu = jnp.mean(xf, axis=-1, keepdims=True)
    xc = xf - mu
    var = jnp.mean(xc * xc, axis=-1, keepdims=True)
    y = xc * lax.rsqrt(var + EPS) * g.astype(jnp.float32) + b.astype(jnp.float32)
    return y.astype(x.dtype)


def causal_dwconv(ext, w):
    return lax.conv_general_dilated(
        ext, w.astype(ext.dtype)[:, None, :], window_strides=(1,), padding="VALID",
        dimension_numbers=("NWC", "WIO", "NWC"), feature_group_count=ext.shape[-1])


def pool_mixer(u, prev, pos0, w_pool, pool_scale):
    b, l, _ = u.shape
    ext = jnp.concatenate([prev.astype(u.dtype), u], axis=1)
    cs = jnp.cumsum(ext.astype(jnp.float32), axis=1)
    cs = jnp.pad(cs, ((0, 0), (1, 0), (0, 0)))
    end = cs[:, POOL_BUF + 1:POOL_BUF + 1 + l]
    pos = pos0 + jnp.arange(l)
    means = []
    for g, w in enumerate(POOL_WINDOWS):
        sl = slice(g * POOL_GW, (g + 1) * POOL_GW)
        start = cs[:, POOL_BUF + 1 - w:POOL_BUF + 1 - w + l, sl]
        cnt = jnp.minimum(w, pos + 1).astype(jnp.float32)[None, :, None]
        means.append((end[..., sl] - start) / cnt)
    mean = jnp.stack(means, axis=2)
    z = (mean - u.reshape(b, l, N_POOL_GROUPS, POOL_GW).astype(jnp.float32)).astype(u.dtype)
    z = jnp.einsum("blgc,gcd->blgd", z, w_pool).reshape(b, l, POOL_W)
    return rmsnorm(z, pool_scale), ext[:, -POOL_BUF:]


def short_gated_conv(b_gate, c_gate, v, prev, conv_w, gain):
    s = c_gate * v
    ext = jnp.concatenate([prev.astype(s.dtype), s], axis=1)
    y = b_gate * causal_dwconv(ext, conv_w)
    return rmsnorm(y, gain), ext[:, -(SCONV_K - 1):]


def conformer_conv(a, gate, prev, conv_w, conv_b, ln_g, ln_b, gain):
    g = a * jax.nn.sigmoid(gate)
    ext = jnp.concatenate([prev.astype(g.dtype), g], axis=1)
    y = causal_dwconv(ext, conv_w) + conv_b
    y = jax.nn.silu(layernorm(y, ln_g, ln_b))
    return rmsnorm(y, gain), ext[:, -(CCONV_K - 1):]


def mixer_sublayer(x, prev_pool, prev_sconv, prev_cconv, pos0, norm_g, w_in, w_pool, pool_scale,
                   sconv_w, sconv_gain, cconv_w, cconv_b, cconv_ln_g, cconv_ln_b, cconv_gain, w_out):
    h = rmsnorm(x, norm_g)
    u = h @ w_in
    u_pool, sc_b, sc_c, sc_v, cc_a, cc_g = jnp.split(u, IN_SPLITS, axis=-1)
    o_a, st_a = pool_mixer(u_pool, prev_pool, pos0, w_pool, pool_scale)
    o_b, st_b = short_gated_conv(sc_b, sc_c, sc_v, prev_sconv, sconv_w, sconv_gain)
    o_c, st_c = conformer_conv(cc_a, cc_g, prev_cconv, cconv_w, cconv_b, cconv_ln_g, cconv_ln_b, cconv_gain)
    o = jnp.concatenate([o_a, o_b, o_c], axis=-1) @ w_out
    return x + o, st_a, st_b, st_c


def memory_kv(mem, g, w_k, w_v):
    b, n, _ = mem.shape
    m = rmsnorm(mem, g)
    k = (m @ w_k).reshape(b, n, X_HEADS, X_HEAD_DIM)
    v = (m @ w_v).reshape(b, n, X_HEADS, X_HEAD_DIM)
    return k, v


def cross_attention(h, k, v, w_q, w_o):
    b, l, _ = h.shape
    q = (h @ w_q).reshape(b, l, X_HEADS, X_HEAD_DIM)
    s = jnp.einsum("blhd,bmhd->bhlm", q, k.astype(q.dtype)).astype(jnp.float32) * (X_HEAD_DIM ** -0.5)
    p = jax.nn.softmax(s, axis=-1).astype(q.dtype)
    o = jnp.einsum("bhlm,bmhd->blhd", p, v.astype(q.dtype)).reshape(b, l, D_MODEL)
    return o @ w_o


def swiglu(h, w_gate, w_up, w_down):
    return (jax.nn.silu(h @ w_gate) * (h @ w_up)) @ w_down


def moe_swiglu(h, router_w, w_gate, w_up, w_down):
    shp = h.shape
    t = h.reshape(-1, shp[-1])
    logits = (t @ router_w).astype(jnp.float32)
    top_v, top_i = lax.top_k(logits, TOP_K)
    gates = jax.nn.softmax(top_v, axis=-1)
    combine = jnp.einsum("tk,tke->te", gates,
                         jax.nn.one_hot(top_i, N_EXPERTS, dtype=jnp.float32)).astype(t.dtype)
    out = jnp.zeros_like(t)
    for e in range(N_EXPERTS):
        out = out + combine[:, e:e + 1] * swiglu(t, w_gate[e], w_up[e], w_down[e])
    return out.reshape(shp)


def setup_inputs(seed: int = 0) -> dict:
    key = jax.random.key(seed)
    ks = iter(jax.random.split(key, 40))

    def nrm(shape, scale):
        return jax.random.normal(next(ks), shape, jnp.float32) * scale

    def gain(shape):
        return 1.0 + 0.02 * jax.random.normal(next(ks), shape, jnp.float32)

    return {
        "x_prompt": nrm((BATCH, SEQ, D_MODEL), 1.0),
        "x_sample": nrm((DEC_BATCH, DEC_SEQ, D_MODEL), 1.0),
        "cache_mem_k": nrm((DEPTH, DEC_BATCH, N_MEM, X_HEADS, X_HEAD_DIM), 1.0),
        "cache_mem_v": nrm((DEPTH, DEC_BATCH, N_MEM, X_HEADS, X_HEAD_DIM), 1.0),
        "state_pool": nrm((DEPTH, DEC_BATCH, POOL_BUF, POOL_W), 1.0),
        "state_sconv": nrm((DEPTH, DEC_BATCH, SCONV_K - 1, SCONV_CH), 1.0),
        "state_cconv": nrm((DEPTH, DEC_BATCH, CCONV_K - 1, CCONV_CH), 0.5),
        "mem_prompt": nrm((BATCH, N_MEM, D_MODEL), 1.0),
        "norm_mix": gain((DEPTH, D_MODEL)),
        "w_in": nrm((DEPTH, D_MODEL, IN_W), D_MODEL ** -0.5),
        "w_pool": nrm((DEPTH, N_POOL_GROUPS, POOL_GW, POOL_GW), POOL_GW ** -0.5),
        "pool_scale": gain((DEPTH, POOL_W)),
        "sconv_w": nrm((DEPTH, SCONV_K, SCONV_CH), SCONV_K ** -0.5),
        "sconv_gain": gain((DEPTH, SCONV_CH)),
        "cconv_w": nrm((DEPTH, CCONV_K, CCONV_CH), CCONV_K ** -0.5),
        "cconv_b": nrm((DEPTH, CCONV_CH), 0.02),
        "cconv_ln_g": gain((DEPTH, CCONV_CH)),
        "cconv_ln_b": nrm((DEPTH, CCONV_CH), 0.02),
        "cconv_gain": gain((DEPTH, CCONV_CH)),
        "w_out": nrm((DEPTH, MIX_W, D_MODEL), MIX_W ** -0.5),
        "norm_x": gain((DEPTH, D_MODEL)),
        "mem_norm": gain((DEPTH, D_MODEL)),
        "w_q": nrm((DEPTH, D_MODEL, D_MODEL), D_MODEL ** -0.5),
        "w_k": nrm((DEPTH, D_MODEL, D_MODEL), D_MODEL ** -0.5),
        "w_v": nrm((DEPTH, D_MODEL, D_MODEL), D_MODEL ** -0.5),
        "w_o": nrm((DEPTH, D_MODEL, D_MODEL), D_MODEL ** -0.5),
        "norm_ffn": gain((DEPTH, D_MODEL)),
        "ffn_w_gate": nrm((N_DENSE, D_MODEL, D_FF), D_MODEL ** -0.5),
        "ffn_w_up": nrm((N_DENSE, D_MODEL, D_FF), D_MODEL ** -0.5),
        "ffn_w_down": nrm((N_DENSE, D_FF, D_MODEL), D_FF ** -0.5),
        "router_w": nrm((N_MOE, D_MODEL, N_EXPERTS), D_MODEL ** -0.5),
        "moe_w_gate": nrm((N_MOE, N_EXPERTS, D_MODEL, D_FF_EXPERT), D_MODEL ** -0.5),
        "moe_w_up": nrm((N_MOE, N_EXPERTS, D_MODEL, D_FF_EXPERT), D_MODEL ** -0.5),
        "moe_w_down": nrm((N_MOE, N_EXPERTS, D_FF_EXPERT, D_MODEL), D_FF_EXPERT ** -0.5),
        "final_norm": gain((D_MODEL,)),
    }


def reference(x_prompt, x_sample, cache_mem_k, cache_mem_v, state_pool, state_sconv, state_cconv,
              mem_prompt, norm_mix, w_in, w_pool, pool_scale, sconv_w, sconv_gain, cconv_w, cconv_b,
              cconv_ln_g, cconv_ln_b, cconv_gain, w_out, norm_x, mem_norm, w_q, w_k, w_v, w_o,
              norm_ffn, ffn_w_gate, ffn_w_up, ffn_w_down, router_w, moe_w_gate, moe_w_up, moe_w_down,
              final_norm):

    def run_layer(i, x, mem_k, mem_v, prev_pool, prev_sconv, prev_cconv, pos0):
        x, s_pool, s_sconv, s_cconv = mixer_sublayer(
            x, prev_pool, prev_sconv, prev_cconv, pos0, norm_mix[i], w_in[i], w_pool[i], pool_scale[i],
            sconv_w[i], sconv_gain[i], cconv_w[i], cconv_b[i], cconv_ln_g[i], cconv_ln_b[i],
            cconv_gain[i], w_out[i])
        x = x + cross_attention(rmsnorm(x, norm_x[i]), mem_k, mem_v, w_q[i], w_o[i])
        h = rmsnorm(x, norm_ffn[i])
        j = i // 2
        if i % 2 == 0:
            x = x + swiglu(h, ffn_w_gate[j], ffn_w_up[j], ffn_w_down[j])
        else:
            x = x + moe_swiglu(h, router_w[j], moe_w_gate[j], moe_w_up[j], moe_w_down[j])
        return x, s_pool, s_sconv, s_cconv

    bp = x_prompt.shape[0]
    dt = x_prompt.dtype
    zero_pool = jnp.zeros((bp, POOL_BUF, POOL_W), dt)
    zero_sconv = jnp.zeros((bp, SCONV_K - 1, SCONV_CH), dt)
    zero_cconv = jnp.zeros((bp, CCONV_K - 1, CCONV_CH), dt)

    xp, xs = x_prompt, x_sample
    mk_p, mv_p, pool_p, pool_s, sc_p, sc_s, cc_p, cc_s = [], [], [], [], [], [], [], []
    for i in range(DEPTH):
        kp, vp = memory_kv(mem_prompt, mem_norm[i], w_k[i], w_v[i])
        xp, a_p, b_p, c_p = run_layer(i, xp, kp, vp, zero_pool, zero_sconv, zero_cconv, 0)
        xs, a_s, b_s, c_s = run_layer(i, xs, cache_mem_k[i], cache_mem_v[i], state_pool[i],
                                      state_sconv[i], state_cconv[i], PAST_LEN)
        mk_p.append(kp)
        mv_p.append(vp)
        pool_p.append(a_p)
        pool_s.append(a_s)
        sc_p.append(b_p)
        sc_s.append(b_s)
        cc_p.append(c_p)
        cc_s.append(c_s)

    y_prompt = rmsnorm(xp, final_norm)
    y_sample = rmsnorm(xs, final_norm)
    return (y_prompt, y_sample, jnp.stack(mk_p), jnp.stack(mv_p), jnp.stack(pool_p), jnp.stack(pool_s),
            jnp.stack(sc_p), jnp.stack(sc_s), jnp.stack(cc_p), jnp.stack(cc_s))
```

```python
import functools

import jax
import jax.numpy as jnp
from jax import lax
from jax.experimental import pallas as pl
from jax.experimental.pallas import tpu as pltpu

F32 = jnp.float32
BF16 = jnp.bfloat16

EPS = 1e-6
POOL_WINDOWS = (2, 4, 8, 16)
SCONV_K = 3
CCONV_K = 31
X_HEADS = 4
N_EXPERTS = 8
TOP_K = 2

V7X_VMEM_BYTES = 64 * 1024 * 1024
VMEM_LIMIT_CAP = 58 * 1024 * 1024
HALO = 32


def _params(n_axes, vmem_bytes):
    limit = int(min(max(vmem_bytes * 5 // 4 + (4 << 20), 32 << 20), VMEM_LIMIT_CAP))
    return pltpu.CompilerParams(dimension_semantics=("arbitrary",) * n_axes,
                                vmem_limit_bytes=limit)


def _pick_tile(n, candidates):
    for c in candidates:
        if n % c == 0:
            return c
    return n


def _sigmoid(x):
    return 1.0 / (1.0 + jnp.exp(-x))


def _rms_rows(x, gain):
    return x * lax.rsqrt(jnp.mean(x * x, axis=-1, keepdims=True) + EPS) * gain


def _rmsnorm_kernel(x_ref, g_ref, o_ref):
    o_ref[...] = _rms_rows(x_ref[...], g_ref[...]).astype(o_ref.dtype)


def rmsnorm(x, gain, out_dtype):
    m, d = x.shape
    bm = _pick_tile(m, (416, 512, 256, 128, 64, 8))
    return pl.pallas_call(
        _rmsnorm_kernel,
        grid=(m // bm,),
        in_specs=[pl.BlockSpec((bm, d), lambda i: (i, 0)),
                  pl.BlockSpec((1, d), lambda i: (0, 0))],
        out_specs=pl.BlockSpec((bm, d), lambda i: (i, 0)),
        out_shape=jax.ShapeDtypeStruct((m, d), out_dtype),
        compiler_params=_params(1, 2 * bm * d * 8),
        name="rmsnorm",
    )(x, gain.reshape(1, d))


def _mm_kernel(*refs, nk, n_w, has_res, grouped, epilogue):
    if grouped:
        te_ref, tf_ref, tv_ref = refs[:3]
        refs = refs[3:]
    a_ref = refs[0]
    w_refs = refs[1:1 + n_w]
    pos = 1 + n_w
    res_ref = refs[pos] if has_res else None
    pos += int(has_res)
    o_ref = refs[pos]
    wb_refs = refs[pos + 1:pos + 1 + n_w]
    acc_ref = refs[pos + 1 + n_w] if nk > 1 else None
    k = pl.program_id(1)
    m = pl.program_id(2)
    first = (tf_ref[m] == 1) if grouped else (m == 0)

    def finish(vals):
        if epilogue == "swiglu":
            g, u = vals
            out = g * _sigmoid(g) * u
        else:
            out = vals[0]
        if has_res:
            out = out + res_ref[...]
        o_ref[...] = out.astype(o_ref.dtype)

    def compute():
        @pl.when(first)
        def _():
            for w_ref, wb_ref in zip(w_refs, wb_refs):
                wb_ref[...] = w_ref[...].astype(BF16)

        a = a_ref[...]
        parts = [jnp.dot(a, wb_ref[...], preferred_element_type=F32) for wb_ref in wb_refs]
        if nk == 1:
            finish(parts)
        else:
            @pl.when(k == 0)
            def _():
                acc_ref[m] = parts[0]

            @pl.when(jnp.logical_and(k > 0, k < nk - 1))
            def _():
                acc_ref[m] += parts[0]

            @pl.when(k == nk - 1)
            def _():
                finish([acc_ref[m] + parts[0]])

    if grouped:
        pl.when(tv_ref[m] == 1)(compute)

        @pl.when(tv_ref[m] == 0)
        def _():
            o_ref[...] = jnp.zeros_like(o_ref)
    else:
        compute()


def matmul(a, ws, *, out_dtype, bm, bn, bk=None, residual=None, epilogue="none",
           group_meta=None, name="matmul"):
    m_rows, k_dim = a.shape
    grouped = group_meta is not None
    n_dim = ws[0].shape[-1]
    bk = k_dim if bk is None else bk
    assert m_rows % bm == 0 and n_dim % bn == 0 and k_dim % bk == 0
    nm, nn, nk = m_rows // bm, n_dim // bn, k_dim // bk
    n_w = len(ws)
    assert nk == 1 or (n_w == 1 and epilogue == "none" and not grouped)
    has_res = residual is not None

    if nk == 1:
        def o_map(n, k, m, *_):
            return (m, n)
    else:
        def o_map(n, k, m, *_):
            return (jnp.where(k == nk - 1, m, 0), n)

    if grouped:
        w_spec = pl.BlockSpec((None, bk, bn), lambda n, k, m, te, tf, tv: (te[m], k, n))
    else:
        w_spec = pl.BlockSpec((bk, bn), lambda n, k, m, *_: (k, n))
    in_specs = [pl.BlockSpec((bm, bk), lambda n, k, m, *_: (m, k))] + [w_spec] * n_w
    operands = [a] + list(ws)
    if has_res:
        in_specs.append(pl.BlockSpec((bm, bn), o_map))
        operands.append(residual)
    scratch = [pltpu.VMEM((bk, bn), BF16) for _ in range(n_w)]
    if nk > 1:
        scratch.append(pltpu.VMEM((nm, bm, bn), F32))
    out_bytes = jnp.dtype(out_dtype).itemsize
    vmem = (2 * bm * bk * 2 + n_w * (2 * bk * bn * 4 + bk * bn * 2) + 2 * bm * bn * out_bytes
            + (2 * bm * bn * 4 if has_res else 0) + (nm * bm * bn * 4 if nk > 1 else 0)
            + n_w * bm * bn * 4)
    kernel = functools.partial(_mm_kernel, nk=nk, n_w=n_w, has_res=has_res, grouped=grouped,
                               epilogue=epilogue)
    grid_spec = pltpu.PrefetchScalarGridSpec(
        num_scalar_prefetch=3 if grouped else 0,
        grid=(nn, nk, nm),
        in_specs=in_specs,
        out_specs=pl.BlockSpec((bm, bn), o_map),
        scratch_shapes=scratch,
    )
    call = pl.pallas_call(
        kernel,
        grid_spec=grid_spec,
        out_shape=jax.ShapeDtypeStruct((m_rows, n_dim), out_dtype),
        compiler_params=_params(3, vmem),
        name=name,
    )
    if grouped:
        return call(*group_meta, *operands)
    return call(*operands)


def _pool_group(x, hist, w, pos, wp_bf16):
    h = hist.shape[0]
    ext = jnp.concatenate([hist, x], axis=0)
    s, off, length = ext, 0, 1
    while length < w:
        s = s[length:] + s[:-length]
        off += length
        length *= 2
    total = s[h - off:h - off + x.shape[0]]
    cnt = jnp.minimum(w, pos + 1).astype(F32)
    z = total / cnt - x
    return jnp.dot(z.astype(BF16), wp_bf16, preferred_element_type=F32)


def _cconv_tail(y, lng, lnb, gain):
    mu = jnp.mean(y, axis=-1, keepdims=True)
    yc = y - mu
    var = jnp.mean(yc * yc, axis=-1, keepdims=True)
    yn = yc * lax.rsqrt(var + EPS) * lng + lnb
    act = yn * _sigmoid(yn)
    return _rms_rows(act, gain)


def _mixer_prompt_kernel(cur_ref, halo_ref, wpool_ref, pscale_ref, scw_ref, scg_ref, ccw_ref,
                         ccb_ref, lng_ref, lnb_ref, ccg_ref, o_ref, stb_ref, stc_ref,
                         ext_ref, y_ref, *, tl, tiles_per_seq, pool_w, sconv_ch, cconv_ch):
    r = pl.program_id(0)
    tpos = r % tiles_per_seq
    first = tpos == 0
    gw = pool_w // len(POOL_WINDOWS)
    c_b = pool_w
    c_c = c_b + sconv_ch
    c_v = c_c + sconv_ch
    c_a = c_v + sconv_ch
    c_g = c_a + cconv_ch

    def hist(lo, c0, c1):
        h = halo_ref[lo:HALO, c0:c1]
        return jnp.where(first, jnp.zeros_like(h), h)

    pos = lax.broadcasted_iota(jnp.int32, (tl, 1), 0) + tpos * tl
    outs = []
    for g, w in enumerate(POOL_WINDOWS):
        c0 = g * gw
        outs.append(_pool_group(cur_ref[:, c0:c0 + gw], hist(HALO - 16, c0, c0 + gw), w, pos,
                                wpool_ref[g].astype(BF16)))
    z = jnp.concatenate(outs, axis=-1)
    o_ref[:, 0:pool_w] = _rms_rows(z, pscale_ref[...]).astype(o_ref.dtype)

    s = cur_ref[:, c_c:c_v] * cur_ref[:, c_v:c_a]
    sh = hist(HALO - 8, c_c, c_v) * hist(HALO - 8, c_v, c_a)
    ext = jnp.concatenate([sh, s], axis=0)
    conv = scw_ref[SCONV_K - 1:SCONV_K, :] * s
    for k in range(SCONV_K - 1):
        lo = 8 - (SCONV_K - 1) + k
        conv = conv + scw_ref[k:k + 1, :] * ext[lo:lo + tl]
    y = cur_ref[:, c_b:c_c] * conv
    o_ref[:, c_b:c_c] = _rms_rows(y, scg_ref[...]).astype(o_ref.dtype)
    stb_ref[...] = s[tl - (SCONV_K - 1):tl]

    glu = cur_ref[:, c_a:c_g] * _sigmoid(cur_ref[:, c_g:c_g + cconv_ch])
    ext_ref[0:HALO, :] = hist(0, c_a, c_g) * _sigmoid(hist(0, c_g, c_g + cconv_ch))
    ext_ref[HALO:HALO + tl, :] = glu
    stc_ref[...] = glu[tl - (CCONV_K - 1):tl]
    rows, lanes = 32, 512
    for r0 in range(0, tl, rows):
        for l0 in range(0, cconv_ch, lanes):
            acc = jnp.broadcast_to(ccb_ref[:, l0:l0 + lanes], (rows, lanes))
            for k in range(CCONV_K):
                lo = HALO - (CCONV_K - 1) + k + r0
                acc = acc + ccw_ref[k:k + 1, l0:l0 + lanes] * ext_ref[lo:lo + rows, l0:l0 + lanes]
            y_ref[r0:r0 + rows, l0:l0 + lanes] = acc
    oc = _cconv_tail(y_ref[...], lng_ref[...], lnb_ref[...], ccg_ref[...])
    o_ref[:, c_c:c_c + cconv_ch] = oc.astype(o_ref.dtype)


def mixer_prompt(u, n_seq, seq_len, wl, dims):
    pool_w, sconv_ch, cconv_ch = dims
    in_w = u.shape[1]
    mix_w = pool_w + sconv_ch + cconv_ch
    tl = _pick_tile(seq_len, (256, 128, 64, 32))
    tiles_per_seq = seq_len // tl
    hb = tl // HALO
    kernel = functools.partial(_mixer_prompt_kernel, tl=tl, tiles_per_seq=tiles_per_seq,
                               pool_w=pool_w, sconv_ch=sconv_ch, cconv_ch=cconv_ch)

    def row(v):
        return v.reshape(1, -1)

    def full(shape):
        return pl.BlockSpec(shape, lambda r: (0,) * len(shape))

    vmem = 2 * tl * in_w * 4 + 2 * HALO * in_w * 4 + 2 * tl * mix_w * 2 + 3 * tl * cconv_ch * 4 \
        + 12 * tl * sconv_ch * 4
    return pl.pallas_call(
        kernel,
        grid=(n_seq * tiles_per_seq,),
        in_specs=[
            pl.BlockSpec((tl, in_w), lambda r: (r, 0)),
            pl.BlockSpec((HALO, in_w), lambda r: (jnp.maximum(r * hb - 1, 0), 0)),
            full(wl["w_pool"].shape), full((1, pool_w)),
            full((SCONV_K, sconv_ch)), full((1, sconv_ch)),
            full((CCONV_K, cconv_ch)), full((1, cconv_ch)), full((1, cconv_ch)),
            full((1, cconv_ch)), full((1, cconv_ch)),
        ],
        out_specs=[
            pl.BlockSpec((tl, mix_w), lambda r: (r, 0)),
            pl.BlockSpec((None, SCONV_K - 1, sconv_ch), lambda r: (r // tiles_per_seq, 0, 0)),
            pl.BlockSpec((None, CCONV_K - 1, cconv_ch), lambda r: (r // tiles_per_seq, 0, 0)),
        ],
        out_shape=[
            jax.ShapeDtypeStruct((n_seq * seq_len, mix_w), BF16),
            jax.ShapeDtypeStruct((n_seq, SCONV_K - 1, sconv_ch), F32),
            jax.ShapeDtypeStruct((n_seq, CCONV_K - 1, cconv_ch), F32),
        ],
        scratch_shapes=[pltpu.VMEM((HALO + tl, cconv_ch), F32), pltpu.VMEM((tl, cconv_ch), F32)],
        compiler_params=_params(1, vmem),
        name="mixer_prompt",
    )(u, u, wl["w_pool"], row(wl["pool_scale"]), wl["sconv_w"], row(wl["sconv_gain"]),
      wl["cconv_w"], row(wl["cconv_b"]), row(wl["cconv_ln_g"]), row(wl["cconv_ln_b"]),
      row(wl["cconv_gain"]))


def _mixer_sample_kernel(u_ref, stp_ref, stb_ref, stc_ref, wpool_ref, pscale_ref, scw_ref,
                         scg_ref, ccw_ref, ccb_ref, lng_ref, lnb_ref, ccg_ref, o_ref, np_ref,
                         nb_ref, nc_ref, *, pool_w, sconv_ch, cconv_ch):
    gw = pool_w // len(POOL_WINDOWS)
    c_b = pool_w
    c_c = c_b + sconv_ch
    c_v = c_c + sconv_ch
    c_a = c_v + sconv_ch
    c_g = c_a + cconv_ch
    pool_buf = stp_ref.shape[0]

    outs = []
    for g, w in enumerate(POOL_WINDOWS):
        c0 = g * gw
        x = u_ref[:, c0:c0 + gw]
        total = x
        for j in range(1, w):
            total = total + stp_ref[pool_buf - j, :, c0:c0 + gw]
        z = total / float(w) - x
        outs.append(jnp.dot(z.astype(BF16), wpool_ref[g].astype(BF16), preferred_element_type=F32))
    z = jnp.concatenate(outs, axis=-1)
    o_ref[:, 0:pool_w] = _rms_rows(z, pscale_ref[...]).astype(o_ref.dtype)
    for k in range(pool_buf - 1):
        np_ref[k] = stp_ref[k + 1]
    np_ref[pool_buf - 1] = u_ref[:, 0:pool_w]

    s = u_ref[:, c_c:c_v] * u_ref[:, c_v:c_a]
    conv = scw_ref[SCONV_K - 1:SCONV_K, :] * s
    for k in range(SCONV_K - 1):
        conv = conv + scw_ref[k:k + 1, :] * stb_ref[k]
    y = u_ref[:, c_b:c_c] * conv
    o_ref[:, c_b:c_c] = _rms_rows(y, scg_ref[...]).astype(o_ref.dtype)
    for k in range(SCONV_K - 2):
        nb_ref[k] = stb_ref[k + 1]
    nb_ref[SCONV_K - 2] = s

    glu = u_ref[:, c_a:c_g] * _sigmoid(u_ref[:, c_g:c_g + cconv_ch])
    acc = ccb_ref[...] + ccw_ref[CCONV_K - 1:CCONV_K, :] * glu
    for k in range(CCONV_K - 1):
        acc = acc + ccw_ref[k:k + 1, :] * stc_ref[k]
    oc = _cconv_tail(acc, lng_ref[...], lnb_ref[...], ccg_ref[...])
    o_ref[:, c_c:c_c + cconv_ch] = oc.astype(o_ref.dtype)
    for k in range(CCONV_K - 2):
        nc_ref[k] = stc_ref[k + 1]
    nc_ref[CCONV_K - 2] = glu


def mixer_sample(u_s, st_pool, st_sconv, st_cconv, wl, dims):
    pool_w, sconv_ch, cconv_ch = dims
    n_seq, in_w = u_s.shape
    mix_w = pool_w + sconv_ch + cconv_ch
    bb = _pick_tile(n_seq, (32, 16, 8))
    kernel = functools.partial(_mixer_sample_kernel, pool_w=pool_w, sconv_ch=sconv_ch,
                               cconv_ch=cconv_ch)

    def row(v):
        return v.reshape(1, -1)

    def full(shape):
        return pl.BlockSpec(shape, lambda i: (0,) * len(shape))

    def state(h, c):
        return pl.BlockSpec((h, bb, c), lambda i: (0, i, 0))

    hp, hb, hc = st_pool.shape[0], st_sconv.shape[0], st_cconv.shape[0]
    vmem = 2 * bb * in_w * 4 + 4 * bb * 4 * (hp * pool_w + hb * sconv_ch + hc * cconv_ch) \
        + 16 * bb * cconv_ch * 4
    return pl.pallas_call(
        kernel,
        grid=(n_seq // bb,),
        in_specs=[
            pl.BlockSpec((bb, in_w), lambda i: (i, 0)),
            state(hp, pool_w), state(hb, sconv_ch), state(hc, cconv_ch),
            full(wl["w_pool"].shape), full((1, pool_w)),
            full((SCONV_K, sconv_ch)), full((1, sconv_ch)),
            full((CCONV_K, cconv_ch)), full((1, cconv_ch)), full((1, cconv_ch)),
            full((1, cconv_ch)), full((1, cconv_ch)),
        ],
        out_specs=[pl.BlockSpec((bb, mix_w), lambda i: (i, 0)),
                   state(hp, pool_w), state(hb, sconv_ch), state(hc, cconv_ch)],
        out_shape=[jax.ShapeDtypeStruct((n_seq, mix_w), BF16),
                   jax.ShapeDtypeStruct(st_pool.shape, F32),
                   jax.ShapeDtypeStruct(st_sconv.shape, F32),
                   jax.ShapeDtypeStruct(st_cconv.shape, F32)],
        compiler_params=_params(1, vmem),
        name="mixer_sample",
    )(u_s, st_pool, st_sconv, st_cconv, wl["w_pool"], row(wl["pool_scale"]), wl["sconv_w"],
      row(wl["sconv_gain"]), wl["cconv_w"], row(wl["cconv_b"]), row(wl["cconv_ln_g"]),
      row(wl["cconv_ln_b"]), row(wl["cconv_gain"]))


def _attn_prompt_kernel(q_ref, k_ref, v_ref, o_ref, *, scale):
    kb = k_ref[...].astype(BF16)
    vb = v_ref[...].astype(BF16)
    s = lax.dot_general(q_ref[...], kb, (((1,), (1,)), ((), ())),
                        preferred_element_type=F32) * scale
    e = jnp.exp(s - jnp.max(s, axis=-1, keepdims=True))
    p = e / jnp.sum(e, axis=-1, keepdims=True)
    o_ref[...] = jnp.dot(p.astype(BF16), vb, preferred_element_type=F32).astype(o_ref.dtype)


def attention_prompt(q, k, v, n_seq, seq_len, n_mem):
    d = q.shape[1]
    hd = d // X_HEADS
    tq = _pick_tile(seq_len, (1024, 512, 256, 128, 64))
    nq = seq_len // tq
    kernel = functools.partial(_attn_prompt_kernel, scale=float(hd) ** -0.5)
    vmem = 4 * tq * hd * 2 + 4 * n_mem * hd * 4 + 4 * n_mem * hd * 2 + 6 * tq * n_mem * 4 + tq * hd * 4
    return pl.pallas_call(
        kernel,
        grid=(n_seq, X_HEADS, nq),
        in_specs=[pl.BlockSpec((tq, hd), lambda b, h, i: (b * nq + i, h)),
                  pl.BlockSpec((n_mem, hd), lambda b, h, i: (b, h)),
                  pl.BlockSpec((n_mem, hd), lambda b, h, i: (b, h))],
        out_specs=pl.BlockSpec((tq, hd), lambda b, h, i: (b * nq + i, h)),
        out_shape=jax.ShapeDtypeStruct((n_seq * seq_len, d), BF16),
        compiler_params=_params(3, vmem),
        name="attn_prompt",
    )(q, k, v)


def _attn_sample_kernel(q_ref, k_ref, v_ref, o_ref, *, scale):
    q = q_ref[...]
    s = jnp.sum(k_ref[...] * q[None], axis=-1, keepdims=True) * scale
    e = jnp.exp(s - jnp.max(s, axis=0, keepdims=True))
    p = e / jnp.sum(e, axis=0, keepdims=True)
    o_ref[...] = jnp.sum(p * v_ref[...], axis=0)


def attention_sample(q_s, cache_k, cache_v, layer):
    n_seq, heads, hd = q_s.shape
    n_mem = cache_k.shape[2]
    kernel = functools.partial(_attn_sample_kernel, scale=float(hd) ** -0.5)
    blk = n_mem * 8 * hd * 4
    cache_spec = pl.BlockSpec((None, None, n_mem, heads, hd), lambda b: (layer, b, 0, 0, 0))
    return pl.pallas_call(
        kernel,
        grid=(n_seq,),
        in_specs=[pl.BlockSpec((None, heads, hd), lambda b: (b, 0, 0)), cache_spec, cache_spec],
        out_specs=pl.BlockSpec((None, heads, hd), lambda b: (b, 0, 0)),
        out_shape=jax.ShapeDtypeStruct((n_seq, heads, hd), F32),
        compiler_params=_params(1, 5 * blk),
        name="attn_sample",
    )(q_s, cache_k, cache_v)


def _router_kernel(x_ref, g_ref, w_ref, idx_ref, gate_ref):
    h = _rms_rows(x_ref[...], g_ref[...])
    logits = jnp.dot(h, w_ref[...], preferred_element_type=F32, precision=lax.Precision.HIGHEST)
    n_e = logits.shape[-1]
    iota = lax.broadcasted_iota(jnp.int32, logits.shape, 1)
    m1 = jnp.max(logits, axis=-1, keepdims=True)
    i1 = jnp.min(jnp.where(logits == m1, iota, n_e), axis=-1, keepdims=True)
    rest = jnp.where(iota == i1, -jnp.inf, logits)
    m2 = jnp.max(rest, axis=-1, keepdims=True)
    i2 = jnp.min(jnp.where(rest == m2, iota, n_e), axis=-1, keepdims=True)
    e2 = jnp.exp(m2 - m1)
    denom = 1.0 + e2
    idx_ref[...] = jnp.concatenate([i1, i2], axis=-1)
    gate_ref[...] = jnp.concatenate([1.0 / denom, e2 / denom], axis=-1)


def router(x, gain, router_w):
    t, d = x.shape
    n_e = router_w.shape[1]
    bt = _pick_tile(t, (416, 512, 256, 128, 64, 8))
    return pl.pallas_call(
        _router_kernel,
        grid=(t // bt,),
        in_specs=[pl.BlockSpec((bt, d), lambda i: (i, 0)),
                  pl.BlockSpec((1, d), lambda i: (0, 0)),
                  pl.BlockSpec((d, n_e), lambda i: (0, 0))],
        out_specs=[pl.BlockSpec((bt, TOP_K), lambda i: (i, 0)),
                   pl.BlockSpec((bt, TOP_K), lambda i: (i, 0))],
        out_shape=[jax.ShapeDtypeStruct((t, TOP_K), jnp.int32),
                   jax.ShapeDtypeStruct((t, TOP_K), F32)],
        compiler_params=_params(1, 4 * bt * d * 4 + 2 * d * 128 * 4),
        name="router",
    )(x, gain.reshape(1, d), router_w)


def _gather_norm_kernel(tok_ref, valid_ref, x_hbm, g_ref, o_ref, buf_ref, sem, *, bm):
    i = pl.program_id(0)

    def row_copy(j):
        tok = tok_ref[i * bm + j]
        return pltpu.make_async_copy(x_hbm.at[pl.ds(tok, 1)], buf_ref.at[pl.ds(j, 1)], sem)

    @pl.when(valid_ref[i] == 1)
    def _():
        def start(j, c):
            row_copy(j).start()
            return c

        def wait(j, c):
            row_copy(j).wait()
            return c

        lax.fori_loop(0, bm, start, 0)
        lax.fori_loop(0, bm, wait, 0)
        o_ref[...] = _rms_rows(buf_ref[...], g_ref[...]).astype(o_ref.dtype)

    @pl.when(valid_ref[i] == 0)
    def _():
        o_ref[...] = jnp.zeros_like(o_ref)


def gather_norm(x, gain, row_token, tile_valid, bm):
    t, d = x.shape
    n_tiles = tile_valid.shape[0]
    kernel = functools.partial(_gather_norm_kernel, bm=bm)
    grid_spec = pltpu.PrefetchScalarGridSpec(
        num_scalar_prefetch=2,
        grid=(n_tiles,),
        in_specs=[pl.BlockSpec(memory_space=pl.ANY),
                  pl.BlockSpec((1, d), lambda i, *_: (0, 0))],
        out_specs=pl.BlockSpec((bm, d), lambda i, *_: (i, 0)),
        scratch_shapes=[pltpu.VMEM((bm, d), F32), pltpu.SemaphoreType.DMA],
    )
    return pl.pallas_call(
        kernel,
        grid_spec=grid_spec,
        out_shape=jax.ShapeDtypeStruct((n_tiles * bm, d), BF16),
        compiler_params=_params(1, bm * d * 4 * 3 + 2 * bm * d * 2),
        name="moe_gather",
    )(row_token, tile_valid, x, gain.reshape(1, d))


def _combine_kernel(pos_ref, x_ref, gate_ref, y_hbm, o_ref, buf_ref, sem, *, bt):
    i = pl.program_id(0)

    def row_copy(j, k):
        p = pos_ref[(i * bt + j) * TOP_K + k]
        return pltpu.make_async_copy(y_hbm.at[pl.ds(p, 1)], buf_ref.at[k, pl.ds(j, 1)], sem)

    def start(j, c):
        for k in range(TOP_K):
            row_copy(j, k).start()
        return c

    def wait(j, c):
        for k in range(TOP_K):
            row_copy(j, k).wait()
        return c

    lax.fori_loop(0, bt, start, 0)
    lax.fori_loop(0, bt, wait, 0)
    out = x_ref[...]
    gates = gate_ref[...]
    for k in range(TOP_K):
        out = out + gates[:, k:k + 1] * buf_ref[k]
    o_ref[...] = out


def moe_combine(x, gates, y, pos_flat):
    t, d = x.shape
    bt = _pick_tile(t, (416, 256, 128, 64, 8))
    kernel = functools.partial(_combine_kernel, bt=bt)
    grid_spec = pltpu.PrefetchScalarGridSpec(
        num_scalar_prefetch=1,
        grid=(t // bt,),
        in_specs=[pl.BlockSpec((bt, d), lambda i, *_: (i, 0)),
                  pl.BlockSpec((bt, TOP_K), lambda i, *_: (i, 0)),
                  pl.BlockSpec(memory_space=pl.ANY)],
        out_specs=pl.BlockSpec((bt, d), lambda i, *_: (i, 0)),
        scratch_shapes=[pltpu.VMEM((TOP_K, bt, d), F32), pltpu.SemaphoreType.DMA],
    )
    return pl.pallas_call(
        kernel,
        grid_spec=grid_spec,
        out_shape=jax.ShapeDtypeStruct((t, d), F32),
        compiler_params=_params(1, bt * d * 4 * (TOP_K + 5)),
        name="moe_combine",
    )(pos_flat, x, gates, y)


def _routing_tables(idx, bm, n_tiles):
    t = idx.shape[0]
    e_flat = idx.reshape(-1)
    onehot = (e_flat[:, None] == jnp.arange(N_EXPERTS, dtype=jnp.int32)[None, :]).astype(jnp.int32)
    rank = jnp.sum((jnp.cumsum(onehot, axis=0) - onehot) * onehot, axis=1)
    counts = jnp.sum(onehot, axis=0)
    tiles_e = (counts + bm - 1) // bm
    tile_end = jnp.cumsum(tiles_e)
    tile_start = tile_end - tiles_e
    pos = (tile_start * bm)[e_flat] + rank
    row_token = jnp.zeros((n_tiles * bm,), jnp.int32).at[pos].set(
        jnp.arange(t * TOP_K, dtype=jnp.int32) // TOP_K)
    tile_ids = jnp.arange(n_tiles, dtype=jnp.int32)
    n_used = tile_end[-1]
    tile_valid = (tile_ids < n_used).astype(jnp.int32)
    clamped = jnp.minimum(tile_ids, n_used - 1)
    tile_expert = jnp.sum((clamped[:, None] >= tile_end[None, :]).astype(jnp.int32), axis=1)
    tile_first = (clamped == tile_start[tile_expert]).astype(jnp.int32) * tile_valid
    tile_first = tile_first.at[0].set(1)
    return pos.astype(jnp.int32), row_token, tile_expert.astype(jnp.int32), tile_first, tile_valid


def _dense_tiles(t):
    return _pick_tile(t, (832, 640, 512, 416, 256, 128, 64, 8))


def kernel(x_prompt, x_sample, cache_mem_k, cache_mem_v, state_pool, state_sconv, state_cconv,
           mem_prompt, norm_mix, w_in, w_pool, pool_scale, sconv_w, sconv_gain, cconv_w, cconv_b,
           cconv_ln_g, cconv_ln_b, cconv_gain, w_out, norm_x, mem_norm, w_q, w_k, w_v, w_o,
           norm_ffn, ffn_w_gate, ffn_w_up, ffn_w_down, router_w, moe_w_gate, moe_w_up,
           moe_w_down, final_norm):
    n_seq, seq_len, d = x_prompt.shape
    n_dec = x_sample.shape[0]
    depth = w_in.shape[0]
    n_mem = mem_prompt.shape[1]
    hd = d // X_HEADS
    pool_w = pool_scale.shape[1]
    sconv_ch = sconv_gain.shape[1]
    cconv_ch = cconv_gain.shape[1]
    dims = (pool_w, sconv_ch, cconv_ch)
    tp = n_seq * seq_len
    t = tp + n_dec
    bm = _dense_tiles(t)
    bn = 512

    x = jnp.concatenate([x_prompt.reshape(tp, d), x_sample.reshape(n_dec, d)], axis=0)
    mem = mem_prompt.reshape(n_seq * n_mem, d)
    bm_mem = _pick_tile(n_seq * n_mem, (1024, 512, 256, 128, 64, 8))

    mem_k, mem_v = [], []
    pool_p, pool_s, sc_p, sc_s, cc_p, cc_s = [], [], [], [], [], []
    for i in range(depth):
        wl = dict(w_pool=w_pool[i], pool_scale=pool_scale[i], sconv_w=sconv_w[i],
                  sconv_gain=sconv_gain[i], cconv_w=cconv_w[i], cconv_b=cconv_b[i],
                  cconv_ln_g=cconv_ln_g[i], cconv_ln_b=cconv_ln_b[i], cconv_gain=cconv_gain[i])

        h = rmsnorm(x, norm_mix[i], BF16)
        u = matmul(h, [w_in[i]], out_dtype=F32, bm=bm, bn=bn, name="w_in")
        o_p, stb_p, stc_p = mixer_prompt(u, n_seq, seq_len, wl, dims)
        o_s, stp_s, stb_s, stc_s = mixer_sample(
            u[tp:], state_pool[i].transpose(1, 0, 2), state_sconv[i].transpose(1, 0, 2),
            state_cconv[i].transpose(1, 0, 2), wl, dims)
        o = jnp.concatenate([o_p, o_s], axis=0)
        x = matmul(o, [w_out[i]], out_dtype=F32, bm=bm, bn=bn, residual=x, name="w_out")
        pool_buf = state_pool.shape[2]
        pool_p.append(u[:tp, :pool_w].reshape(n_seq, seq_len, pool_w)[:, seq_len - pool_buf:])
        pool_s.append(stp_s.transpose(1, 0, 2))
        sc_p.append(stb_p)
        sc_s.append(stb_s.transpose(1, 0, 2))
        cc_p.append(stc_p)
        cc_s.append(stc_s.transpose(1, 0, 2))

        mn = rmsnorm(mem, mem_norm[i], BF16)
        k_p = matmul(mn, [w_k[i]], out_dtype=F32, bm=bm_mem, bn=bn, name="w_k")
        v_p = matmul(mn, [w_v[i]], out_dtype=F32, bm=bm_mem, bn=bn, name="w_v")
        mem_k.append(k_p.reshape(n_seq, n_mem, X_HEADS, hd))
        mem_v.append(v_p.reshape(n_seq, n_mem, X_HEADS, hd))
        hq = rmsnorm(x, norm_x[i], BF16)
        q = matmul(hq, [w_q[i]], out_dtype=BF16, bm=bm, bn=bn, name="w_q")
        a_p = attention_prompt(q, k_p, v_p, n_seq, seq_len, n_mem)
        a_s = attention_sample(q[tp:].astype(F32).reshape(n_dec, X_HEADS, hd),
                               cache_mem_k, cache_mem_v, i)
        a = jnp.concatenate([a_p, a_s.reshape(n_dec, d).astype(BF16)], axis=0)
        x = matmul(a, [w_o[i]], out_dtype=F32, bm=bm, bn=bn, residual=x, name="w_o")

        j = i // 2
        if i % 2 == 0:
            hf = rmsnorm(x, norm_ffn[i], BF16)
            d_ff = ffn_w_gate.shape[2]
            mid = matmul(hf, [ffn_w_gate[j], ffn_w_up[j]], out_dtype=BF16, bm=bm,
                         bn=_pick_tile(d_ff, (256, 128)), epilogue="swiglu", name="ffn_gate_up")
            bk = _pick_tile(d_ff, (5504, 4096, 2048, 1024, 512, 256, 128))
            x = matmul(mid, [ffn_w_down[j]], out_dtype=F32, bm=bm, bn=256, bk=bk, residual=x,
                       name="ffn_down")
        else:
            bmg = 256
            n_tiles = -(-(t * TOP_K) // bmg) + N_EXPERTS
            idx, gates = router(x, norm_ffn[i], router_w[j])
            pos, row_token, tile_expert, tile_first, tile_valid = _routing_tables(idx, bmg, n_tiles)
            meta = (tile_expert, tile_first, tile_valid)
            xs = gather_norm(x, norm_ffn[i], row_token, tile_valid, bmg)
            mid = matmul(xs, [moe_w_gate[j], moe_w_up[j]], out_dtype=BF16, bm=bmg, bn=512,
                         epilogue="swiglu", group_meta=meta, name="moe_gate_up")
            y = matmul(mid, [moe_w_down[j]], out_dtype=F32, bm=bmg, bn=512, group_meta=meta,
                       name="moe_down")
            x = moe_combine(x, gates, y, pos)

    y_all = rmsnorm(x, final_norm, F32)
    y_prompt = y_all[:tp].reshape(n_seq, seq_len, d)
    y_sample = y_all[tp:].reshape(n_dec, 1, d)
    return (y_prompt, y_sample, jnp.stack(mem_k), jnp.stack(mem_v), jnp.stack(pool_p),
            jnp.stack(pool_s), jnp.stack(sc_p), jnp.stack(sc_s), jnp.stack(cc_p), jnp.stack(cc_s))
```

```python
import functools

import jax
import jax.numpy as jnp
from jax import lax
from jax.experimental import pallas as pl
from jax.experimental.pallas import tpu as pltpu

F32 = jnp.float32
BF16 = jnp.bfloat16

EPS = 1e-6
POOL_WINDOWS = (2, 4, 8, 16)
SCONV_K = 3
CCONV_K = 31
X_HEADS = 4
N_EXPERTS = 8
TOP_K = 2

V7X_VMEM_BYTES = 64 * 1024 * 1024
VMEM_LIMIT_CAP = 58 * 1024 * 1024
HALO = 32
MOE_ROW_TILE = 256


def _params(n_axes, vmem_bytes):
    limit = int(min(max(vmem_bytes * 5 // 4 + (4 << 20), 32 << 20), VMEM_LIMIT_CAP))
    return pltpu.CompilerParams(dimension_semantics=("arbitrary",) * n_axes,
                                vmem_limit_bytes=limit)


def _pick_tile(n, candidates):
    for c in candidates:
        if n % c == 0:
            return c
    return n


def _sigmoid(x):
    return 1.0 / (1.0 + jnp.exp(-x))


def _rms_rows(x, gain):
    return x * lax.rsqrt(jnp.mean(x * x, axis=-1, keepdims=True) + EPS) * gain


def _rmsnorm_kernel(x_ref, g_ref, o_ref):
    o_ref[...] = _rms_rows(x_ref[...], g_ref[...]).astype(o_ref.dtype)


def rmsnorm(x, gain, out_dtype):
    m, d = x.shape
    bm = _pick_tile(m, (416, 512, 256, 128, 64, 8))
    return pl.pallas_call(
        _rmsnorm_kernel,
        grid=(m // bm,),
        in_specs=[pl.BlockSpec((bm, d), lambda i: (i, 0)),
                  pl.BlockSpec((1, d), lambda i: (0, 0))],
        out_specs=pl.BlockSpec((bm, d), lambda i: (i, 0)),
        out_shape=jax.ShapeDtypeStruct((m, d), out_dtype),
        compiler_params=_params(1, 2 * bm * d * 8),
        name="rmsnorm",
    )(x, gain.reshape(1, d))


def _mm_kernel(gs_ref, a_hbm, *refs, n_w, has_res, epilogue, bm, bn, k0, kc, n_groups,
               total_tiles):
    w_refs = refs[:n_w]
    pos = n_w
    res_hbm = refs[pos] if has_res else None
    pos += int(has_res)
    o_hbm, a_buf = refs[pos], refs[pos + 1]
    wb_refs = refs[pos + 2:pos + 2 + n_w]
    pos += 2 + n_w
    o_buf = refs[pos]
    r_buf = refs[pos + 1] if has_res else None
    pos += 1 + int(has_res)
    a_sem, o_sem = refs[pos], refs[pos + 1]
    r_sem = refs[pos + 2] if has_res else None

    n = pl.program_id(0)
    g = pl.program_id(1)
    t0 = gs_ref[g]
    nt = gs_ref[g + 1] - t0

    def in_copies(tile, slot, step_n):
        row0 = pl.multiple_of(tile * bm, bm)
        cps = [pltpu.make_async_copy(a_hbm.at[pl.ds(row0, bm), pl.ds(k0, kc)], a_buf.at[slot],
                                     a_sem.at[slot])]
        if has_res:
            col0 = pl.multiple_of(step_n * bn, bn)
            cps.append(pltpu.make_async_copy(res_hbm.at[pl.ds(row0, bm), pl.ds(col0, bn)],
                                             r_buf.at[slot], r_sem.at[slot]))
        return cps

    def out_copy(tile, slot):
        row0 = pl.multiple_of(tile * bm, bm)
        col0 = pl.multiple_of(n * bn, bn)
        return pltpu.make_async_copy(o_buf.at[slot], o_hbm.at[pl.ds(row0, bm), pl.ds(col0, bn)],
                                     o_sem.at[slot])

    @pl.when(jnp.logical_and(jnp.logical_and(n == 0, g == 0), nt > 0))
    def _():
        for cp in in_copies(t0, 0, n):
            cp.start()

    for w_ref, wb_ref in zip(w_refs, wb_refs):
        wb_ref[...] = w_ref[...].astype(BF16)

    def body(i, carry):
        slot = lax.rem(i, 2)
        for cp in in_copies(t0 + i, slot, n):
            cp.wait()

        @pl.when(i + 1 < nt)
        def _():
            for cp in in_copies(t0 + i + 1, 1 - slot, n):
                cp.start()

        a = a_buf[slot]
        parts = [jnp.dot(a, wb_ref[...], preferred_element_type=F32) for wb_ref in wb_refs]
        if epilogue == "swiglu":
            out = parts[0] * _sigmoid(parts[0]) * parts[1]
        else:
            out = parts[0]
        if has_res:
            out = out + r_buf[slot]

        @pl.when(i >= 2)
        def _():
            out_copy(t0 + i - 2, slot).wait()

        o_buf[slot] = out.astype(o_buf.dtype)
        out_copy(t0 + i, slot).start()
        return carry

    lax.fori_loop(0, nt, body, 0)

    @pl.when(nt >= 2)
    def _():
        out_copy(t0 + nt - 2, lax.rem(nt, 2)).wait()

    @pl.when(nt >= 1)
    def _():
        out_copy(t0 + nt - 1, lax.rem(nt + 1, 2)).wait()

    last_g = g == n_groups - 1
    if total_tiles is not None:
        used = gs_ref[n_groups]

        @pl.when(jnp.logical_and(last_g, used < total_tiles))
        def _():
            o_buf[0] = jnp.zeros(o_buf.shape[1:], o_buf.dtype)

            def fill(t, carry):
                cp = out_copy(t, 0)
                cp.start()
                cp.wait()
                return carry

            lax.fori_loop(used, total_tiles, fill, 0)

    g2 = jnp.where(last_g, 0, g + 1)
    n2 = jnp.where(last_g, n + 1, n)
    t2 = gs_ref[g2]
    nt2 = gs_ref[g2 + 1] - t2

    @pl.when(jnp.logical_and(n2 < pl.num_programs(0), nt2 > 0))
    def _():
        for cp in in_copies(t2, 0, n2):
            cp.start()


def matmul(a, ws, w_index, *, out_dtype, bm, bn, residual=None, epilogue="none",
           group_starts=None, k_window=None, w_buffers=2, name="matmul"):
    m_rows = a.shape[0]
    k_full, n_dim = ws[0].shape[-2:]
    kb, kn = k_window if k_window is not None else (0, 1)
    kc = k_full // kn
    assert k_full % kn == 0 and a.shape[1] == k_full
    assert m_rows % bm == 0 and n_dim % bn == 0
    grouped = group_starts is not None
    n_w = len(ws)
    has_res = residual is not None
    total_tiles = m_rows // bm
    if grouped:
        n_groups = group_starts.shape[0] - 1
    else:
        n_groups = 1
        group_starts = jnp.array([0, total_tiles], jnp.int32)

    lead = tuple(w_index)
    if grouped:
        w_map = lambda n, g, gs: lead + (g, kb, n)
    else:
        w_map = lambda n, g, gs: lead + (kb, n)
    w_block = (None,) * (len(lead) + int(grouped)) + (kc, bn)
    w_kwargs = {} if w_buffers == 2 else {"pipeline_mode": pl.Buffered(w_buffers)}
    w_spec = pl.BlockSpec(w_block, w_map, **w_kwargs)
    any_spec = pl.BlockSpec(memory_space=pl.ANY)

    in_specs = [any_spec] + [w_spec] * n_w + ([any_spec] if has_res else [])
    operands = [a] + list(ws) + ([residual] if has_res else [])
    out_bytes = jnp.dtype(out_dtype).itemsize
    scratch = [pltpu.VMEM((2, bm, kc), BF16)]
    scratch += [pltpu.VMEM((kc, bn), BF16) for _ in range(n_w)]
    scratch += [pltpu.VMEM((2, bm, bn), out_dtype)]
    if has_res:
        scratch += [pltpu.VMEM((2, bm, bn), F32)]
    scratch += [pltpu.SemaphoreType.DMA((2,)), pltpu.SemaphoreType.DMA((2,))]
    if has_res:
        scratch += [pltpu.SemaphoreType.DMA((2,))]
    vmem = (2 * bm * kc * 2 + n_w * (w_buffers * kc * bn * 4 + kc * bn * 2)
            + 2 * bm * bn * out_bytes + (2 * bm * bn * 4 if has_res else 0)
            + (n_w + 1) * bm * bn * 4)
    kernel = functools.partial(
        _mm_kernel, n_w=n_w, has_res=has_res, epilogue=epilogue, bm=bm, bn=bn, k0=kb * kc, kc=kc,
        n_groups=n_groups, total_tiles=total_tiles if grouped else None)
    grid_spec = pltpu.PrefetchScalarGridSpec(
        num_scalar_prefetch=1,
        grid=(n_dim // bn, n_groups),
        in_specs=in_specs,
        out_specs=any_spec,
        scratch_shapes=scratch,
    )
    return pl.pallas_call(
        kernel,
        grid_spec=grid_spec,
        out_shape=jax.ShapeDtypeStruct((m_rows, n_dim), out_dtype),
        compiler_params=_params(2, vmem),
        name=name,
    )(group_starts, *operands)


def _pool_group(x, hist, w, pos, wp_bf16):
    h = hist.shape[0]
    ext = jnp.concatenate([hist, x], axis=0)
    s, off, length = ext, 0, 1
    while length < w:
        s = s[length:] + s[:-length]
        off += length
        length *= 2
    total = s[h - off:h - off + x.shape[0]]
    cnt = jnp.minimum(w, pos + 1).astype(F32)
    z = total / cnt - x
    return jnp.dot(z.astype(BF16), wp_bf16, preferred_element_type=F32)


def _cconv_tail(y, lng, lnb, gain):
    mu = jnp.mean(y, axis=-1, keepdims=True)
    yc = y - mu
    var = jnp.mean(yc * yc, axis=-1, keepdims=True)
    yn = yc * lax.rsqrt(var + EPS) * lng + lnb
    act = yn * _sigmoid(yn)
    return _rms_rows(act, gain)


def _mixer_prompt_kernel(cur_ref, halo_ref, wpool_ref, pscale_ref, scw_ref, scg_ref, ccw_ref,
                         ccb_ref, lng_ref, lnb_ref, ccg_ref, o_ref, stb_ref, stc_ref,
                         ext_ref, y_ref, shift_ref, *, tl, tiles_per_seq, pool_w, sconv_ch,
                         cconv_ch):
    r = pl.program_id(0)
    tpos = r % tiles_per_seq
    first = tpos == 0
    gw = pool_w // len(POOL_WINDOWS)
    c_b = pool_w
    c_c = c_b + sconv_ch
    c_v = c_c + sconv_ch
    c_a = c_v + sconv_ch
    c_g = c_a + cconv_ch

    rows = 32

    def before(r0, n, c0, c1):
        if r0 > 0:
            return cur_ref[r0 - n:r0, c0:c1]
        h = halo_ref[HALO - n:HALO, c0:c1]
        return jnp.where(first, jnp.zeros_like(h), h)

    for r0 in range(0, tl, rows):
        r1 = r0 + rows
        pos = lax.broadcasted_iota(jnp.int32, (rows, 1), 0) + (tpos * tl + r0)
        outs = []
        for g, w in enumerate(POOL_WINDOWS):
            c0 = g * gw
            outs.append(_pool_group(cur_ref[r0:r1, c0:c0 + gw], before(r0, 16, c0, c0 + gw), w, pos,
                                    wpool_ref[g].astype(BF16)))
        z = jnp.concatenate(outs, axis=-1)
        o_ref[r0:r1, 0:pool_w] = _rms_rows(z, pscale_ref[...]).astype(o_ref.dtype)

        s = cur_ref[r0:r1, c_c:c_v] * cur_ref[r0:r1, c_v:c_a]
        ext = jnp.concatenate([before(r0, 8, c_c, c_v) * before(r0, 8, c_v, c_a), s], axis=0)
        conv = scw_ref[SCONV_K - 1:SCONV_K, :] * s
        for k in range(SCONV_K - 1):
            lo = 8 - (SCONV_K - 1) + k
            conv = conv + scw_ref[k:k + 1, :] * ext[lo:lo + rows]
        y = cur_ref[r0:r1, c_b:c_c] * conv
        o_ref[r0:r1, c_b:c_c] = _rms_rows(y, scg_ref[...]).astype(o_ref.dtype)
        if r1 == tl:
            stb_ref[...] = s[rows - (SCONV_K - 1):rows]

        ext_ref[HALO + r0:HALO + r1, :] = \
            cur_ref[r0:r1, c_a:c_g] * _sigmoid(cur_ref[r0:r1, c_g:c_g + cconv_ch])

    ext_ref[0:HALO, :] = before(0, HALO, c_a, c_g) * _sigmoid(before(0, HALO, c_g, c_g + cconv_ch))
    stc_ref[...] = ext_ref[HALO + tl - (CCONV_K - 1):HALO + tl, :]
    lanes = 512
    lo0 = HALO - (CCONV_K - 1)
    n_shift = tl + HALO - 8
    fill = 40 if n_shift % 40 == 0 else 8
    for s in range(1, 8):
        for i0 in range(0, n_shift, fill):
            for l0 in range(0, cconv_ch, lanes):
                shift_ref[s - 1, i0:i0 + fill, l0:l0 + lanes] = \
                    ext_ref[i0 + s:i0 + s + fill, l0:l0 + lanes]
    for r0 in range(0, tl, rows):
        for l0 in range(0, cconv_ch, lanes):
            acc = jnp.broadcast_to(ccb_ref[:, l0:l0 + lanes], (rows, lanes))
            for k in range(CCONV_K):
                s = (lo0 + k) % 8
                base = lo0 + k - s + r0
                if s == 0:
                    tap = ext_ref[base:base + rows, l0:l0 + lanes]
                else:
                    tap = shift_ref[s - 1, base:base + rows, l0:l0 + lanes]
                acc = acc + ccw_ref[k:k + 1, l0:l0 + lanes] * tap
            y_ref[r0:r0 + rows, l0:l0 + lanes] = acc
        oc = _cconv_tail(y_ref[r0:r0 + rows, :], lng_ref[...], lnb_ref[...], ccg_ref[...])
        o_ref[r0:r0 + rows, c_c:c_c + cconv_ch] = oc.astype(o_ref.dtype)


def mixer_prompt(u, n_seq, seq_len, wl, dims):
    pool_w, sconv_ch, cconv_ch = dims
    in_w = u.shape[1]
    mix_w = pool_w + sconv_ch + cconv_ch
    tl = _pick_tile(seq_len, (256, 128, 64, 32))
    tiles_per_seq = seq_len // tl
    hb = tl // HALO
    kernel = functools.partial(_mixer_prompt_kernel, tl=tl, tiles_per_seq=tiles_per_seq,
                               pool_w=pool_w, sconv_ch=sconv_ch, cconv_ch=cconv_ch)

    def row(v):
        return v.reshape(1, -1)

    def full(shape):
        return pl.BlockSpec(shape, lambda r: (0,) * len(shape))

    vmem = 2 * tl * in_w * 4 + 2 * HALO * in_w * 4 + 2 * tl * mix_w * 2 + 3 * tl * cconv_ch * 4 \
        + 12 * tl * sconv_ch * 4 + 7 * (tl + HALO - 8) * cconv_ch * 4
    return pl.pallas_call(
        kernel,
        grid=(n_seq * tiles_per_seq,),
        in_specs=[
            pl.BlockSpec((tl, in_w), lambda r: (r, 0)),
            pl.BlockSpec((HALO, in_w), lambda r: (jnp.maximum(r * hb - 1, 0), 0)),
            full(wl["w_pool"].shape), full((1, pool_w)),
            full((SCONV_K, sconv_ch)), full((1, sconv_ch)),
            full((CCONV_K, cconv_ch)), full((1, cconv_ch)), full((1, cconv_ch)),
            full((1, cconv_ch)), full((1, cconv_ch)),
        ],
        out_specs=[
            pl.BlockSpec((tl, mix_w), lambda r: (r, 0)),
            pl.BlockSpec((None, SCONV_K - 1, sconv_ch), lambda r: (r // tiles_per_seq, 0, 0)),
            pl.BlockSpec((None, CCONV_K - 1, cconv_ch), lambda r: (r // tiles_per_seq, 0, 0)),
        ],
        out_shape=[
            jax.ShapeDtypeStruct((n_seq * seq_len, mix_w), BF16),
            jax.ShapeDtypeStruct((n_seq, SCONV_K - 1, sconv_ch), F32),
            jax.ShapeDtypeStruct((n_seq, CCONV_K - 1, cconv_ch), F32),
        ],
        scratch_shapes=[pltpu.VMEM((HALO + tl, cconv_ch), F32), pltpu.VMEM((tl, cconv_ch), F32),
                        pltpu.VMEM((7, tl + HALO - 8, cconv_ch), F32)],
        compiler_params=_params(1, vmem),
        name="mixer_prompt",
    )(u, u, wl["w_pool"], row(wl["pool_scale"]), wl["sconv_w"], row(wl["sconv_gain"]),
      wl["cconv_w"], row(wl["cconv_b"]), row(wl["cconv_ln_g"]), row(wl["cconv_ln_b"]),
      row(wl["cconv_gain"]))


def _mixer_sample_kernel(u_ref, stp_ref, stb_ref, stc_ref, wpool_ref, pscale_ref, scw_ref,
                         scg_ref, ccw_ref, ccb_ref, lng_ref, lnb_ref, ccg_ref, o_ref, np_ref,
                         nb_ref, nc_ref, *, pool_w, sconv_ch, cconv_ch):
    gw = pool_w // len(POOL_WINDOWS)
    c_b = pool_w
    c_c = c_b + sconv_ch
    c_v = c_c + sconv_ch
    c_a = c_v + sconv_ch
    c_g = c_a + cconv_ch
    pool_buf = stp_ref.shape[0]

    outs = []
    for g, w in enumerate(POOL_WINDOWS):
        c0 = g * gw
        x = u_ref[:, c0:c0 + gw]
        total = x
        for j in range(1, w):
            total = total + stp_ref[pool_buf - j, :, c0:c0 + gw]
        z = total / float(w) - x
        outs.append(jnp.dot(z.astype(BF16), wpool_ref[g].astype(BF16), preferred_element_type=F32))
    z = jnp.concatenate(outs, axis=-1)
    o_ref[:, 0:pool_w] = _rms_rows(z, pscale_ref[...]).astype(o_ref.dtype)
    for k in range(pool_buf - 1):
        np_ref[k] = stp_ref[k + 1]
    np_ref[pool_buf - 1] = u_ref[:, 0:pool_w]

    s = u_ref[:, c_c:c_v] * u_ref[:, c_v:c_a]
    conv = scw_ref[SCONV_K - 1:SCONV_K, :] * s
    for k in range(SCONV_K - 1):
        conv = conv + scw_ref[k:k + 1, :] * stb_ref[k]
    y = u_ref[:, c_b:c_c] * conv
    o_ref[:, c_b:c_c] = _rms_rows(y, scg_ref[...]).astype(o_ref.dtype)
    for k in range(SCONV_K - 2):
        nb_ref[k] = stb_ref[k + 1]
    nb_ref[SCONV_K - 2] = s

    glu = u_ref[:, c_a:c_g] * _sigmoid(u_ref[:, c_g:c_g + cconv_ch])
    acc = ccb_ref[...] + ccw_ref[CCONV_K - 1:CCONV_K, :] * glu
    for k in range(CCONV_K - 1):
        acc = acc + ccw_ref[k:k + 1, :] * stc_ref[k]
    oc = _cconv_tail(acc, lng_ref[...], lnb_ref[...], ccg_ref[...])
    o_ref[:, c_c:c_c + cconv_ch] = oc.astype(o_ref.dtype)
    for k in range(CCONV_K - 2):
        nc_ref[k] = stc_ref[k + 1]
    nc_ref[CCONV_K - 2] = glu


def mixer_sample(u_s, st_pool, st_sconv, st_cconv, wl, dims):
    pool_w, sconv_ch, cconv_ch = dims
    n_seq, in_w = u_s.shape
    mix_w = pool_w + sconv_ch + cconv_ch
    bb = _pick_tile(n_seq, (32, 16, 8))
    kernel = functools.partial(_mixer_sample_kernel, pool_w=pool_w, sconv_ch=sconv_ch,
                               cconv_ch=cconv_ch)

    def row(v):
        return v.reshape(1, -1)

    def full(shape):
        return pl.BlockSpec(shape, lambda i: (0,) * len(shape))

    def state(h, c):
        return pl.BlockSpec((h, bb, c), lambda i: (0, i, 0))

    hp, hb, hc = st_pool.shape[0], st_sconv.shape[0], st_cconv.shape[0]
    vmem = 2 * bb * in_w * 4 + 4 * bb * 4 * (hp * pool_w + hb * sconv_ch + hc * cconv_ch) \
        + 16 * bb * cconv_ch * 4
    return pl.pallas_call(
        kernel,
        grid=(n_seq // bb,),
        in_specs=[
            pl.BlockSpec((bb, in_w), lambda i: (i, 0)),
            state(hp, pool_w), state(hb, sconv_ch), state(hc, cconv_ch),
            full(wl["w_pool"].shape), full((1, pool_w)),
            full((SCONV_K, sconv_ch)), full((1, sconv_ch)),
            full((CCONV_K, cconv_ch)), full((1, cconv_ch)), full((1, cconv_ch)),
            full((1, cconv_ch)), full((1, cconv_ch)),
        ],
        out_specs=[pl.BlockSpec((bb, mix_w), lambda i: (i, 0)),
                   state(hp, pool_w), state(hb, sconv_ch), state(hc, cconv_ch)],
        out_shape=[jax.ShapeDtypeStruct((n_seq, mix_w), BF16),
                   jax.ShapeDtypeStruct(st_pool.shape, F32),
                   jax.ShapeDtypeStruct(st_sconv.shape, F32),
                   jax.ShapeDtypeStruct(st_cconv.shape, F32)],
        compiler_params=_params(1, vmem),
        name="mixer_sample",
    )(u_s, st_pool, st_sconv, st_cconv, wl["w_pool"], row(wl["pool_scale"]), wl["sconv_w"],
      row(wl["sconv_gain"]), wl["cconv_w"], row(wl["cconv_b"]), row(wl["cconv_ln_g"]),
      row(wl["cconv_ln_b"]), row(wl["cconv_gain"]))


def _attn_prompt_kernel(q_ref, k_ref, v_ref, o_ref, *, scale):
    kb = k_ref[...].astype(BF16)
    vb = v_ref[...].astype(BF16)
    s = lax.dot_general(q_ref[...], kb, (((1,), (1,)), ((), ())),
                        preferred_element_type=F32) * scale
    e = jnp.exp(s - jnp.max(s, axis=-1, keepdims=True))
    p = e / jnp.sum(e, axis=-1, keepdims=True)
    o_ref[...] = jnp.dot(p.astype(BF16), vb, preferred_element_type=F32).astype(o_ref.dtype)


def attention_prompt(q, k, v, n_seq, seq_len, n_mem):
    d = q.shape[1]
    hd = d // X_HEADS
    tq = _pick_tile(seq_len, (1024, 512, 256, 128, 64))
    nq = seq_len // tq
    kernel = functools.partial(_attn_prompt_kernel, scale=float(hd) ** -0.5)
    vmem = 4 * tq * hd * 2 + 4 * n_mem * hd * 4 + 4 * n_mem * hd * 2 + 6 * tq * n_mem * 4 + tq * hd * 4
    return pl.pallas_call(
        kernel,
        grid=(n_seq, X_HEADS, nq),
        in_specs=[pl.BlockSpec((tq, hd), lambda b, h, i: (b * nq + i, h)),
                  pl.BlockSpec((n_mem, hd), lambda b, h, i: (b, h)),
                  pl.BlockSpec((n_mem, hd), lambda b, h, i: (b, h))],
        out_specs=pl.BlockSpec((tq, hd), lambda b, h, i: (b * nq + i, h)),
        out_shape=jax.ShapeDtypeStruct((n_seq * seq_len, d), BF16),
        compiler_params=_params(3, vmem),
        name="attn_prompt",
    )(q, k, v)


def _attn_sample_kernel(q_ref, k_ref, v_ref, o_ref, *, scale):
    q = q_ref[...]
    s = jnp.sum(k_ref[...] * q[None], axis=-1, keepdims=True) * scale
    e = jnp.exp(s - jnp.max(s, axis=0, keepdims=True))
    p = e / jnp.sum(e, axis=0, keepdims=True)
    o_ref[...] = jnp.sum(p * v_ref[...], axis=0)


def attention_sample(q_s, cache_k, cache_v, layer):
    n_seq, heads, hd = q_s.shape
    n_mem = cache_k.shape[2]
    kernel = functools.partial(_attn_sample_kernel, scale=float(hd) ** -0.5)
    blk = n_mem * 8 * hd * 4
    cache_spec = pl.BlockSpec((None, None, n_mem, heads, hd), lambda b: (layer, b, 0, 0, 0))
    return pl.pallas_call(
        kernel,
        grid=(n_seq,),
        in_specs=[pl.BlockSpec((None, heads, hd), lambda b: (b, 0, 0)), cache_spec, cache_spec],
        out_specs=pl.BlockSpec((None, heads, hd), lambda b: (b, 0, 0)),
        out_shape=jax.ShapeDtypeStruct((n_seq, heads, hd), F32),
        compiler_params=_params(1, 5 * blk),
        name="attn_sample",
    )(q_s, cache_k, cache_v)


def _router_kernel(x_ref, g_ref, w_ref, idx_ref, gate_ref):
    h = _rms_rows(x_ref[...], g_ref[...])
    logits = jnp.dot(h, w_ref[...], preferred_element_type=F32, precision=lax.Precision.HIGHEST)
    n_e = logits.shape[-1]
    iota = lax.broadcasted_iota(jnp.int32, logits.shape, 1)
    m1 = jnp.max(logits, axis=-1, keepdims=True)
    i1 = jnp.min(jnp.where(logits == m1, iota, n_e), axis=-1, keepdims=True)
    rest = jnp.where(iota == i1, -jnp.inf, logits)
    m2 = jnp.max(rest, axis=-1, keepdims=True)
    i2 = jnp.min(jnp.where(rest == m2, iota, n_e), axis=-1, keepdims=True)
    e2 = jnp.exp(m2 - m1)
    denom = 1.0 + e2
    idx_ref[...] = jnp.concatenate([i1, i2], axis=-1)
    gate_ref[...] = jnp.concatenate([1.0 / denom, e2 / denom], axis=-1)


def router(x, gain, router_w, layer):
    t, d = x.shape
    n_e = router_w.shape[-1]
    bt = _pick_tile(t, (416, 512, 256, 128, 64, 8))
    return pl.pallas_call(
        _router_kernel,
        grid=(t // bt,),
        in_specs=[pl.BlockSpec((bt, d), lambda i: (i, 0)),
                  pl.BlockSpec((1, d), lambda i: (0, 0)),
                  pl.BlockSpec((None, d, n_e), lambda i: (layer, 0, 0))],
        out_specs=[pl.BlockSpec((bt, TOP_K), lambda i: (i, 0)),
                   pl.BlockSpec((bt, TOP_K), lambda i: (i, 0))],
        out_shape=[jax.ShapeDtypeStruct((t, TOP_K), jnp.int32),
                   jax.ShapeDtypeStruct((t, TOP_K), F32)],
        compiler_params=_params(1, 4 * bt * d * 4 + 2 * d * 128 * 4),
        name="router",
    )(x, gain.reshape(1, d), router_w)


def _gather_norm_kernel(tok_ref, used_ref, x_hbm, g_ref, o_ref, buf_ref, sem, *, bm):
    i = pl.program_id(0)
    used = used_ref[0]

    def row_copy(tile, j, slot):
        tok = tok_ref[tile * bm + j]
        return pltpu.make_async_copy(x_hbm.at[pl.ds(tok, 1)], buf_ref.at[slot, pl.ds(j, 1)],
                                     sem.at[slot])

    def request(tile, slot):
        def start(j, c):
            row_copy(tile, j, slot).start()
            return c

        lax.fori_loop(0, bm, start, 0, unroll=8)

    @pl.when(jnp.logical_and(i == 0, used > 0))
    def _():
        request(0, 0)

    @pl.when(i + 1 < used)
    def _():
        request(i + 1, lax.rem(i + 1, 2))

    @pl.when(i < used)
    def _():
        slot = lax.rem(i, 2)

        def wait(j, c):
            row_copy(i, j, slot).wait()
            return c

        lax.fori_loop(0, bm, wait, 0, unroll=8)
        o_ref[...] = _rms_rows(buf_ref[slot], g_ref[...]).astype(o_ref.dtype)

    @pl.when(i >= used)
    def _():
        o_ref[...] = jnp.zeros_like(o_ref)


def gather_norm(x, gain, row_token, used_tiles, bm, n_tiles):
    t, d = x.shape
    kernel = functools.partial(_gather_norm_kernel, bm=bm)
    grid_spec = pltpu.PrefetchScalarGridSpec(
        num_scalar_prefetch=2,
        grid=(n_tiles,),
        in_specs=[pl.BlockSpec(memory_space=pl.ANY),
                  pl.BlockSpec((1, d), lambda i, *_: (0, 0))],
        out_specs=pl.BlockSpec((bm, d), lambda i, *_: (i, 0)),
        scratch_shapes=[pltpu.VMEM((2, bm, d), F32), pltpu.SemaphoreType.DMA((2,))],
    )
    return pl.pallas_call(
        kernel,
        grid_spec=grid_spec,
        out_shape=jax.ShapeDtypeStruct((n_tiles * bm, d), BF16),
        compiler_params=_params(1, bm * d * 4 * 4 + 2 * bm * d * 2),
        name="moe_gather",
    )(row_token, used_tiles, x, gain.reshape(1, d))


def _combine_kernel(pos_ref, x_ref, gate_ref, y_hbm, o_ref, buf_ref, sem, *, bt):
    i = pl.program_id(0)

    def row_copy(j, k):
        p = pos_ref[(i * bt + j) * TOP_K + k]
        return pltpu.make_async_copy(y_hbm.at[pl.ds(p, 1)], buf_ref.at[k, pl.ds(j, 1)], sem)

    def start(j, c):
        for k in range(TOP_K):
            row_copy(j, k).start()
        return c

    def wait(j, c):
        for k in range(TOP_K):
            row_copy(j, k).wait()
        return c

    lax.fori_loop(0, bt, start, 0, unroll=8)
    lax.fori_loop(0, bt, wait, 0, unroll=8)
    out = x_ref[...]
    gates = gate_ref[...]
    for k in range(TOP_K):
        out = out + gates[:, k:k + 1] * buf_ref[k]
    o_ref[...] = out


def moe_combine(x, gates, y, pos_flat):
    t, d = x.shape
    bt = _pick_tile(t, (416, 256, 128, 64, 8))
    kernel = functools.partial(_combine_kernel, bt=bt)
    grid_spec = pltpu.PrefetchScalarGridSpec(
        num_scalar_prefetch=1,
        grid=(t // bt,),
        in_specs=[pl.BlockSpec((bt, d), lambda i, *_: (i, 0)),
                  pl.BlockSpec((bt, TOP_K), lambda i, *_: (i, 0)),
                  pl.BlockSpec(memory_space=pl.ANY)],
        out_specs=pl.BlockSpec((bt, d), lambda i, *_: (i, 0)),
        scratch_shapes=[pltpu.VMEM((TOP_K, bt, d), F32), pltpu.SemaphoreType.DMA],
    )
    return pl.pallas_call(
        kernel,
        grid_spec=grid_spec,
        out_shape=jax.ShapeDtypeStruct((t, d), F32),
        compiler_params=_params(1, bt * d * 4 * (TOP_K + 5)),
        name="moe_combine",
    )(pos_flat, x, gates, y)


def _routing_tables(idx, bm, n_tiles):
    t = idx.shape[0]
    e_flat = idx.reshape(-1)
    onehot = (e_flat[:, None] == jnp.arange(N_EXPERTS, dtype=jnp.int32)[None, :]).astype(jnp.int32)
    rank = jnp.sum((jnp.cumsum(onehot, axis=0) - onehot) * onehot, axis=1)
    counts = jnp.sum(onehot, axis=0)
    tiles_e = (counts + bm - 1) // bm
    group_starts = jnp.concatenate([jnp.zeros((1,), jnp.int32),
                                    jnp.cumsum(tiles_e).astype(jnp.int32)])
    pos = (group_starts[:-1] * bm)[e_flat] + rank
    row_token = jnp.zeros((n_tiles * bm,), jnp.int32).at[pos].set(
        jnp.arange(t * TOP_K, dtype=jnp.int32) // TOP_K)
    return pos.astype(jnp.int32), row_token, group_starts


def _tiles(t):
    return dict(
        bm=_pick_tile(t, (832, 640, 512, 416, 256, 128, 64, 8)),
        bm_long_k=_pick_tile(t, (416, 320, 256, 128, 64, 8)),
        bn=512,
        bn_pair=256,
    )


def kernel(x_prompt, x_sample, cache_mem_k, cache_mem_v, state_pool, state_sconv, state_cconv,
           mem_prompt, norm_mix, w_in, w_pool, pool_scale, sconv_w, sconv_gain, cconv_w, cconv_b,
           cconv_ln_g, cconv_ln_b, cconv_gain, w_out, norm_x, mem_norm, w_q, w_k, w_v, w_o,
           norm_ffn, ffn_w_gate, ffn_w_up, ffn_w_down, router_w, moe_w_gate, moe_w_up,
           moe_w_down, final_norm):
    n_seq, seq_len, d = x_prompt.shape
    n_dec = x_sample.shape[0]
    depth = w_in.shape[0]
    n_mem = mem_prompt.shape[1]
    hd = d // X_HEADS
    pool_w = pool_scale.shape[1]
    sconv_ch = sconv_gain.shape[1]
    cconv_ch = cconv_gain.shape[1]
    dims = (pool_w, sconv_ch, cconv_ch)
    tp = n_seq * seq_len
    t = tp + n_dec
    tiles = _tiles(t)
    bm, bn = tiles["bm"], tiles["bn"]

    x = jnp.concatenate([x_prompt.reshape(tp, d), x_sample.reshape(n_dec, d)], axis=0)
    mem = mem_prompt.reshape(n_seq * n_mem, d)
    bm_mem = _pick_tile(n_seq * n_mem, (512, 256, 128, 64, 8))

    mem_k, mem_v = [], []
    pool_p, pool_s, sc_p, sc_s, cc_p, cc_s = [], [], [], [], [], []
    for i in range(depth):
        wl = dict(w_pool=w_pool[i], pool_scale=pool_scale[i], sconv_w=sconv_w[i],
                  sconv_gain=sconv_gain[i], cconv_w=cconv_w[i], cconv_b=cconv_b[i],
                  cconv_ln_g=cconv_ln_g[i], cconv_ln_b=cconv_ln_b[i], cconv_gain=cconv_gain[i])

        h = rmsnorm(x, norm_mix[i], BF16)
        u = matmul(h, [w_in], (i,), out_dtype=F32, bm=bm, bn=bn, name="w_in")
        o_p, stb_p, stc_p = mixer_prompt(u, n_seq, seq_len, wl, dims)
        o_s, stp_s, stb_s, stc_s = mixer_sample(
            u[tp:], state_pool[i].transpose(1, 0, 2), state_sconv[i].transpose(1, 0, 2),
            state_cconv[i].transpose(1, 0, 2), wl, dims)
        o = jnp.concatenate([o_p, o_s], axis=0)
        x = matmul(o, [w_out], (i,), out_dtype=F32, bm=bm, bn=bn, residual=x, name="w_out")
        pool_buf = state_pool.shape[2]
        pool_p.append(u[:tp, :pool_w].reshape(n_seq, seq_len, pool_w)[:, seq_len - pool_buf:])
        pool_s.append(stp_s.transpose(1, 0, 2))
        sc_p.append(stb_p)
        sc_s.append(stb_s.transpose(1, 0, 2))
        cc_p.append(stc_p)
        cc_s.append(stc_s.transpose(1, 0, 2))

        mn = rmsnorm(mem, mem_norm[i], BF16)
        k_p = matmul(mn, [w_k], (i,), out_dtype=F32, bm=bm_mem, bn=bn, name="w_k")
        v_p = matmul(mn, [w_v], (i,), out_dtype=F32, bm=bm_mem, bn=bn, name="w_v")
        mem_k.append(k_p.reshape(n_seq, n_mem, X_HEADS, hd))
        mem_v.append(v_p.reshape(n_seq, n_mem, X_HEADS, hd))
        hq = rmsnorm(x, norm_x[i], BF16)
        q = matmul(hq, [w_q], (i,), out_dtype=BF16, bm=bm, bn=bn, name="w_q")
        a_p = attention_prompt(q, k_p, v_p, n_seq, seq_len, n_mem)
        a_s = attention_sample(q[tp:].astype(F32).reshape(n_dec, X_HEADS, hd),
                               cache_mem_k, cache_mem_v, i)
        a = jnp.concatenate([a_p, a_s.reshape(n_dec, d).astype(BF16)], axis=0)
        x = matmul(a, [w_o], (i,), out_dtype=F32, bm=bm, bn=bn, residual=x, name="w_o")

        j = i // 2
        if i % 2 == 0:
            hf = rmsnorm(x, norm_ffn[i], BF16)
            mid = matmul(hf, [ffn_w_gate, ffn_w_up], (j,), out_dtype=BF16, bm=bm,
                         bn=tiles["bn_pair"], epilogue="swiglu", name="ffn_gate_up")
            for kb in range(2):
                x = matmul(mid, [ffn_w_down], (j,), out_dtype=F32, bm=tiles["bm_long_k"], bn=bn,
                           residual=x, k_window=(kb, 2), name="ffn_down")
        else:
            bmg = MOE_ROW_TILE
            n_tiles = -(-(t * TOP_K) // bmg) + N_EXPERTS
            idx, gates = router(x, norm_ffn[i], router_w, j)
            pos, row_token, group_starts = _routing_tables(idx, bmg, n_tiles)
            xs = gather_norm(x, norm_ffn[i], row_token, group_starts[N_EXPERTS:], bmg, n_tiles)
            mid = matmul(xs, [moe_w_gate, moe_w_up], (j,), out_dtype=BF16, bm=bmg, bn=bn,
                         epilogue="swiglu", group_starts=group_starts, name="moe_gate_up")
            y = matmul(mid, [moe_w_down], (j,), out_dtype=F32, bm=bmg, bn=bn,
                       group_starts=group_starts, name="moe_down")
            x = moe_combine(x, gates, y, pos)

    y_all = rmsnorm(x, final_norm, F32)
    y_prompt = y_all[:tp].reshape(n_seq, seq_len, d)
    y_sample = y_all[tp:].reshape(n_dec, 1, d)
    return (y_prompt, y_sample, jnp.stack(mem_k), jnp.stack(mem_v), jnp.stack(pool_p),
            jnp.stack(pool_s), jnp.stack(sc_p), jnp.stack(sc_s), jnp.stack(cc_p), jnp.stack(cc_s))
```

```python
import functools

import jax
import jax.numpy as jnp
from jax import lax
from jax.experimental import pallas as pl
from jax.experimental.pallas import tpu as pltpu

F32 = jnp.float32
BF16 = jnp.bfloat16

EPS = 1e-6
POOL_WINDOWS = (2, 4, 8, 16)
SCONV_K = 3
CCONV_K = 31
X_HEADS = 4
N_EXPERTS = 8
TOP_K = 2

V7X_VMEM_BYTES = 64 * 1024 * 1024
VMEM_LIMIT_CAP = 58 * 1024 * 1024
HALO = 32
MOE_ROW_TILE = 256


def _params(n_axes, vmem_bytes):
    limit = int(min(max(vmem_bytes * 5 // 4 + (4 << 20), 32 << 20), VMEM_LIMIT_CAP))
    return pltpu.CompilerParams(dimension_semantics=("arbitrary",) * n_axes,
                                vmem_limit_bytes=limit)


def _pick_tile(n, candidates):
    for c in candidates:
        if n % c == 0:
            return c
    return n


def _sigmoid(x):
    return 1.0 / (1.0 + jnp.exp(-x))


def _rms_rows(x, gain):
    return x * lax.rsqrt(jnp.mean(x * x, axis=-1, keepdims=True) + EPS) * gain


def _rmsnorm_kernel(x_ref, g_ref, o_ref):
    o_ref[...] = _rms_rows(x_ref[...], g_ref[...]).astype(o_ref.dtype)


def rmsnorm(x, gain, out_dtype):
    m, d = x.shape
    bm = _pick_tile(m, (416, 512, 256, 128, 64, 8))
    return pl.pallas_call(
        _rmsnorm_kernel,
        grid=(m // bm,),
        in_specs=[pl.BlockSpec((bm, d), lambda i: (i, 0)),
                  pl.BlockSpec((1, d), lambda i: (0, 0))],
        out_specs=pl.BlockSpec((bm, d), lambda i: (i, 0)),
        out_shape=jax.ShapeDtypeStruct((m, d), out_dtype),
        compiler_params=_params(1, 2 * bm * d * 8),
        name="rmsnorm",
    )(x, gain.reshape(1, d))


def _mm_kernel(gs_ref, a_hbm, *refs, n_w, has_res, epilogue, bm, bn, k0, kc, n_groups,
               total_tiles):
    w_refs = refs[:n_w]
    pos = n_w
    res_hbm = refs[pos] if has_res else None
    pos += int(has_res)
    o_hbm, a_buf = refs[pos], refs[pos + 1]
    wb_refs = refs[pos + 2:pos + 2 + n_w]
    pos += 2 + n_w
    o_buf = refs[pos]
    r_buf = refs[pos + 1] if has_res else None
    pos += 1 + int(has_res)
    a_sem, o_sem = refs[pos], refs[pos + 1]
    r_sem = refs[pos + 2] if has_res else None

    n = pl.program_id(0)
    g = pl.program_id(1)
    t0 = gs_ref[g]
    nt = gs_ref[g + 1] - t0

    def in_copies(tile, slot, step_n):
        row0 = pl.multiple_of(tile * bm, bm)
        cps = [pltpu.make_async_copy(a_hbm.at[pl.ds(row0, bm), pl.ds(k0, kc)], a_buf.at[slot],
                                     a_sem.at[slot])]
        if has_res:
            col0 = pl.multiple_of(step_n * bn, bn)
            cps.append(pltpu.make_async_copy(res_hbm.at[pl.ds(row0, bm), pl.ds(col0, bn)],
                                             r_buf.at[slot], r_sem.at[slot]))
        return cps

    def out_copy(tile, slot):
        row0 = pl.multiple_of(tile * bm, bm)
        col0 = pl.multiple_of(n * bn, bn)
        return pltpu.make_async_copy(o_buf.at[slot], o_hbm.at[pl.ds(row0, bm), pl.ds(col0, bn)],
                                     o_sem.at[slot])

    @pl.when(jnp.logical_and(jnp.logical_and(n == 0, g == 0), nt > 0))
    def _():
        for cp in in_copies(t0, 0, n):
            cp.start()

    for w_ref, wb_ref in zip(w_refs, wb_refs):
        wb_ref[...] = w_ref[...].astype(BF16)

    def body(i, carry):
        slot = lax.rem(i, 2)
        for cp in in_copies(t0 + i, slot, n):
            cp.wait()

        @pl.when(i + 1 < nt)
        def _():
            for cp in in_copies(t0 + i + 1, 1 - slot, n):
                cp.start()

        a = a_buf[slot]
        parts = [jnp.dot(a, wb_ref[...], preferred_element_type=F32) for wb_ref in wb_refs]
        if epilogue == "swiglu":
            out = parts[0] * _sigmoid(parts[0]) * parts[1]
        else:
            out = parts[0]
        if has_res:
            out = out + r_buf[slot]

        @pl.when(i >= 2)
        def _():
            out_copy(t0 + i - 2, slot).wait()

        o_buf[slot] = out.astype(o_buf.dtype)
        out_copy(t0 + i, slot).start()
        return carry

    lax.fori_loop(0, nt, body, 0)

    @pl.when(nt >= 2)
    def _():
        out_copy(t0 + nt - 2, lax.rem(nt, 2)).wait()

    @pl.when(nt >= 1)
    def _():
        out_copy(t0 + nt - 1, lax.rem(nt + 1, 2)).wait()

    last_g = g == n_groups - 1
    if total_tiles is not None:
        used = gs_ref[n_groups]

        @pl.when(jnp.logical_and(last_g, used < total_tiles))
        def _():
            o_buf[0] = jnp.zeros(o_buf.shape[1:], o_buf.dtype)

            def fill(t, carry):
                cp = out_copy(t, 0)
                cp.start()
                cp.wait()
                return carry

            lax.fori_loop(used, total_tiles, fill, 0)

    g2 = jnp.where(last_g, 0, g + 1)
    n2 = jnp.where(last_g, n + 1, n)
    t2 = gs_ref[g2]
    nt2 = gs_ref[g2 + 1] - t2

    @pl.when(jnp.logical_and(n2 < pl.num_programs(0), nt2 > 0))
    def _():
        for cp in in_copies(t2, 0, n2):
            cp.start()


def matmul(a, ws, w_index, *, out_dtype, bm, bn, residual=None, epilogue="none",
           group_starts=None, k_window=None, w_buffers=2, name="matmul"):
    m_rows = a.shape[0]
    k_full, n_dim = ws[0].shape[-2:]
    kb, kn = k_window if k_window is not None else (0, 1)
    kc = k_full // kn
    assert k_full % kn == 0 and a.shape[1] == k_full
    assert m_rows % bm == 0 and n_dim % bn == 0
    grouped = group_starts is not None
    n_w = len(ws)
    has_res = residual is not None
    total_tiles = m_rows // bm
    if grouped:
        n_groups = group_starts.shape[0] - 1
    else:
        n_groups = 1
        group_starts = jnp.array([0, total_tiles], jnp.int32)

    lead = tuple(w_index)
    if grouped:
        w_map = lambda n, g, gs: lead + (g, kb, n)
    else:
        w_map = lambda n, g, gs: lead + (kb, n)
    w_block = (None,) * (len(lead) + int(grouped)) + (kc, bn)
    w_kwargs = {} if w_buffers == 2 else {"pipeline_mode": pl.Buffered(w_buffers)}
    w_spec = pl.BlockSpec(w_block, w_map, **w_kwargs)
    any_spec = pl.BlockSpec(memory_space=pl.ANY)

    in_specs = [any_spec] + [w_spec] * n_w + ([any_spec] if has_res else [])
    operands = [a] + list(ws) + ([residual] if has_res else [])
    out_bytes = jnp.dtype(out_dtype).itemsize
    scratch = [pltpu.VMEM((2, bm, kc), BF16)]
    scratch += [pltpu.VMEM((kc, bn), BF16) for _ in range(n_w)]
    scratch += [pltpu.VMEM((2, bm, bn), out_dtype)]
    if has_res:
        scratch += [pltpu.VMEM((2, bm, bn), F32)]
    scratch += [pltpu.SemaphoreType.DMA((2,)), pltpu.SemaphoreType.DMA((2,))]
    if has_res:
        scratch += [pltpu.SemaphoreType.DMA((2,))]
    vmem = (2 * bm * kc * 2 + n_w * (w_buffers * kc * bn * 4 + kc * bn * 2)
            + 2 * bm * bn * out_bytes + (2 * bm * bn * 4 if has_res else 0)
            + (n_w + 1) * bm * bn * 4)
    kernel = functools.partial(
        _mm_kernel, n_w=n_w, has_res=has_res, epilogue=epilogue, bm=bm, bn=bn, k0=kb * kc, kc=kc,
        n_groups=n_groups, total_tiles=total_tiles if grouped else None)
    grid_spec = pltpu.PrefetchScalarGridSpec(
        num_scalar_prefetch=1,
        grid=(n_dim // bn, n_groups),
        in_specs=in_specs,
        out_specs=any_spec,
        scratch_shapes=scratch,
    )
    return pl.pallas_call(
        kernel,
        grid_spec=grid_spec,
        out_shape=jax.ShapeDtypeStruct((m_rows, n_dim), out_dtype),
        compiler_params=_params(2, vmem),
        name=name,
    )(group_starts, *operands)


def _pool_group(x, hist, w, pos, wp_bf16):
    h = hist.shape[0]
    ext = jnp.concatenate([hist, x], axis=0)
    s, off, length = ext, 0, 1
    while length < w:
        s = s[length:] + s[:-length]
        off += length
        length *= 2
    total = s[h - off:h - off + x.shape[0]]
    cnt = jnp.minimum(w, pos + 1).astype(F32)
    z = total / cnt - x
    return jnp.dot(z.astype(BF16), wp_bf16, preferred_element_type=F32)


def _cconv_tail(y, lng, lnb, gain):
    mu = jnp.mean(y, axis=-1, keepdims=True)
    yc = y - mu
    var = jnp.mean(yc * yc, axis=-1, keepdims=True)
    yn = yc * lax.rsqrt(var + EPS) * lng + lnb
    act = yn * _sigmoid(yn)
    return _rms_rows(act, gain)


def _mixer_prompt_kernel(cur_ref, halo_ref, wpool_ref, pscale_ref, scw_ref, scg_ref, ccw_ref,
                         ccb_ref, lng_ref, lnb_ref, ccg_ref, o_ref, stb_ref, stc_ref,
                         ext_ref, y_ref, shift_ref, *, tl, tiles_per_seq, pool_w, sconv_ch,
                         cconv_ch):
    r = pl.program_id(0)
    tpos = r % tiles_per_seq
    first = tpos == 0
    gw = pool_w // len(POOL_WINDOWS)
    c_b = pool_w
    c_c = c_b + sconv_ch
    c_v = c_c + sconv_ch
    c_a = c_v + sconv_ch
    c_g = c_a + cconv_ch

    rows = 32

    def before(r0, n, c0, c1):
        if r0 > 0:
            return cur_ref[r0 - n:r0, c0:c1]
        h = halo_ref[HALO - n:HALO, c0:c1]
        return jnp.where(first, jnp.zeros_like(h), h)

    for r0 in range(0, tl, rows):
        r1 = r0 + rows
        pos = lax.broadcasted_iota(jnp.int32, (rows, 1), 0) + (tpos * tl + r0)
        outs = []
        for g, w in enumerate(POOL_WINDOWS):
            c0 = g * gw
            outs.append(_pool_group(cur_ref[r0:r1, c0:c0 + gw], before(r0, 16, c0, c0 + gw), w, pos,
                                    wpool_ref[g].astype(BF16)))
        z = jnp.concatenate(outs, axis=-1)
        o_ref[r0:r1, 0:pool_w] = _rms_rows(z, pscale_ref[...]).astype(o_ref.dtype)

        s = cur_ref[r0:r1, c_c:c_v] * cur_ref[r0:r1, c_v:c_a]
        ext = jnp.concatenate([before(r0, 8, c_c, c_v) * before(r0, 8, c_v, c_a), s], axis=0)
        conv = scw_ref[SCONV_K - 1:SCONV_K, :] * s
        for k in range(SCONV_K - 1):
            lo = 8 - (SCONV_K - 1) + k
            conv = conv + scw_ref[k:k + 1, :] * ext[lo:lo + rows]
        y = cur_ref[r0:r1, c_b:c_c] * conv
        o_ref[r0:r1, c_b:c_c] = _rms_rows(y, scg_ref[...]).astype(o_ref.dtype)
        if r1 == tl:
            stb_ref[...] = s[rows - (SCONV_K - 1):rows]

        ext_ref[HALO + r0:HALO + r1, :] = \
            cur_ref[r0:r1, c_a:c_g] * _sigmoid(cur_ref[r0:r1, c_g:c_g + cconv_ch])

    ext_ref[0:HALO, :] = before(0, HALO, c_a, c_g) * _sigmoid(before(0, HALO, c_g, c_g + cconv_ch))
    stc_ref[...] = ext_ref[HALO + tl - (CCONV_K - 1):HALO + tl, :]
    lanes = 512
    lo0 = HALO - (CCONV_K - 1)
    n_shift = tl + HALO - 8
    fill = 40 if n_shift % 40 == 0 else 8
    for s in range(1, 8):
        for i0 in range(0, n_shift, fill):
            for l0 in range(0, cconv_ch, lanes):
                shift_ref[s - 1, i0:i0 + fill, l0:l0 + lanes] = \
                    ext_ref[i0 + s:i0 + s + fill, l0:l0 + lanes]
    for r0 in range(0, tl, rows):
        for l0 in range(0, cconv_ch, lanes):
            acc = jnp.broadcast_to(ccb_ref[:, l0:l0 + lanes], (rows, lanes))
            for k in range(CCONV_K):
                s = (lo0 + k) % 8
                base = lo0 + k - s + r0
                if s == 0:
                    tap = ext_ref[base:base + rows, l0:l0 + lanes]
                else:
                    tap = shift_ref[s - 1, base:base + rows, l0:l0 + lanes]
                acc = acc + ccw_ref[k:k + 1, l0:l0 + lanes] * tap
            y_ref[r0:r0 + rows, l0:l0 + lanes] = acc
        oc = _cconv_tail(y_ref[r0:r0 + rows, :], lng_ref[...], lnb_ref[...], ccg_ref[...])
        o_ref[r0:r0 + rows, c_c:c_c + cconv_ch] = oc.astype(o_ref.dtype)


def mixer_prompt(u, n_seq, seq_len, wl, dims):
    pool_w, sconv_ch, cconv_ch = dims
    in_w = u.shape[1]
    mix_w = pool_w + sconv_ch + cconv_ch
    tl = _pick_tile(seq_len, (256, 128, 64, 32))
    tiles_per_seq = seq_len // tl
    hb = tl // HALO
    kernel = functools.partial(_mixer_prompt_kernel, tl=tl, tiles_per_seq=tiles_per_seq,
                               pool_w=pool_w, sconv_ch=sconv_ch, cconv_ch=cconv_ch)

    def row(v):
        return v.reshape(1, -1)

    def full(shape):
        return pl.BlockSpec(shape, lambda r: (0,) * len(shape))

    vmem = 2 * tl * in_w * 4 + 2 * HALO * in_w * 4 + 2 * tl * mix_w * 2 + 3 * tl * cconv_ch * 4 \
        + 12 * tl * sconv_ch * 4 + 7 * (tl + HALO - 8) * cconv_ch * 4
    return pl.pallas_call(
        kernel,
        grid=(n_seq * tiles_per_seq,),
        in_specs=[
            pl.BlockSpec((tl, in_w), lambda r: (r, 0)),
            pl.BlockSpec((HALO, in_w), lambda r: (jnp.maximum(r * hb - 1, 0), 0)),
            full(wl["w_pool"].shape), full((1, pool_w)),
            full((SCONV_K, sconv_ch)), full((1, sconv_ch)),
            full((CCONV_K, cconv_ch)), full((1, cconv_ch)), full((1, cconv_ch)),
            full((1, cconv_ch)), full((1, cconv_ch)),
        ],
        out_specs=[
            pl.BlockSpec((tl, mix_w), lambda r: (r, 0)),
            pl.BlockSpec((None, SCONV_K - 1, sconv_ch), lambda r: (r // tiles_per_seq, 0, 0)),
            pl.BlockSpec((None, CCONV_K - 1, cconv_ch), lambda r: (r // tiles_per_seq, 0, 0)),
        ],
        out_shape=[
            jax.ShapeDtypeStruct((n_seq * seq_len, mix_w), BF16),
            jax.ShapeDtypeStruct((n_seq, SCONV_K - 1, sconv_ch), F32),
            jax.ShapeDtypeStruct((n_seq, CCONV_K - 1, cconv_ch), F32),
        ],
        scratch_shapes=[pltpu.VMEM((HALO + tl, cconv_ch), F32), pltpu.VMEM((tl, cconv_ch), F32),
                        pltpu.VMEM((7, tl + HALO - 8, cconv_ch), F32)],
        compiler_params=_params(1, vmem),
        name="mixer_prompt",
    )(u, u, wl["w_pool"], row(wl["pool_scale"]), wl["sconv_w"], row(wl["sconv_gain"]),
      wl["cconv_w"], row(wl["cconv_b"]), row(wl["cconv_ln_g"]), row(wl["cconv_ln_b"]),
      row(wl["cconv_gain"]))


def _mixer_sample_kernel(u_ref, stp_ref, stb_ref, stc_ref, wpool_ref, pscale_ref, scw_ref,
                         scg_ref, ccw_ref, ccb_ref, lng_ref, lnb_ref, ccg_ref, o_ref, np_ref,
                         nb_ref, nc_ref, *, pool_w, sconv_ch, cconv_ch):
    gw = pool_w // len(POOL_WINDOWS)
    c_b = pool_w
    c_c = c_b + sconv_ch
    c_v = c_c + sconv_ch
    c_a = c_v + sconv_ch
    c_g = c_a + cconv_ch
    pool_buf = stp_ref.shape[0]

    outs = []
    for g, w in enumerate(POOL_WINDOWS):
        c0 = g * gw
        x = u_ref[:, c0:c0 + gw]
        total = x
        for j in range(1, w):
            total = total + stp_ref[pool_buf - j, :, c0:c0 + gw]
        z = total / float(w) - x
        outs.append(jnp.dot(z.astype(BF16), wpool_ref[g].astype(BF16), preferred_element_type=F32))
    z = jnp.concatenate(outs, axis=-1)
    o_ref[:, 0:pool_w] = _rms_rows(z, pscale_ref[...]).astype(o_ref.dtype)
    for k in range(pool_buf - 1):
        np_ref[k] = stp_ref[k + 1]
    np_ref[pool_buf - 1] = u_ref[:, 0:pool_w]

    s = u_ref[:, c_c:c_v] * u_ref[:, c_v:c_a]
    conv = scw_ref[SCONV_K - 1:SCONV_K, :] * s
    for k in range(SCONV_K - 1):
        conv = conv + scw_ref[k:k + 1, :] * stb_ref[k]
    y = u_ref[:, c_b:c_c] * conv
    o_ref[:, c_b:c_c] = _rms_rows(y, scg_ref[...]).astype(o_ref.dtype)
    for k in range(SCONV_K - 2):
        nb_ref[k] = stb_ref[k + 1]
    nb_ref[SCONV_K - 2] = s

    glu = u_ref[:, c_a:c_g] * _sigmoid(u_ref[:, c_g:c_g + cconv_ch])
    acc = ccb_ref[...] + ccw_ref[CCONV_K - 1:CCONV_K, :] * glu
    for k in range(CCONV_K - 1):
        acc = acc + ccw_ref[k:k + 1, :] * stc_ref[k]
    oc = _cconv_tail(acc, lng_ref[...], lnb_ref[...], ccg_ref[...])
    o_ref[:, c_c:c_c + cconv_ch] = oc.astype(o_ref.dtype)
    for k in range(CCONV_K - 2):
        nc_ref[k] = stc_ref[k + 1]
    nc_ref[CCONV_K - 2] = glu


def mixer_sample(u_s, st_pool, st_sconv, st_cconv, wl, dims):
    pool_w, sconv_ch, cconv_ch = dims
    n_seq, in_w = u_s.shape
    mix_w = pool_w + sconv_ch + cconv_ch
    bb = _pick_tile(n_seq, (32, 16, 8))
    kernel = functools.partial(_mixer_sample_kernel, pool_w=pool_w, sconv_ch=sconv_ch,
                               cconv_ch=cconv_ch)

    def row(v):
        return v.reshape(1, -1)

    def full(shape):
        return pl.BlockSpec(shape, lambda i: (0,) * len(shape))

    def state(h, c):
        return pl.BlockSpec((h, bb, c), lambda i: (0, i, 0))

    hp, hb, hc = st_pool.shape[0], st_sconv.shape[0], st_cconv.shape[0]
    vmem = 2 * bb * in_w * 4 + 4 * bb * 4 * (hp * pool_w + hb * sconv_ch + hc * cconv_ch) \
        + 16 * bb * cconv_ch * 4
    return pl.pallas_call(
        kernel,
        grid=(n_seq // bb,),
        in_specs=[
            pl.BlockSpec((bb, in_w), lambda i: (i, 0)),
            state(hp, pool_w), state(hb, sconv_ch), state(hc, cconv_ch),
            full(wl["w_pool"].shape), full((1, pool_w)),
            full((SCONV_K, sconv_ch)), full((1, sconv_ch)),
            full((CCONV_K, cconv_ch)), full((1, cconv_ch)), full((1, cconv_ch)),
            full((1, cconv_ch)), full((1, cconv_ch)),
        ],
        out_specs=[pl.BlockSpec((bb, mix_w), lambda i: (i, 0)),
                   state(hp, pool_w), state(hb, sconv_ch), state(hc, cconv_ch)],
        out_shape=[jax.ShapeDtypeStruct((n_seq, mix_w), BF16),
                   jax.ShapeDtypeStruct(st_pool.shape, F32),
                   jax.ShapeDtypeStruct(st_sconv.shape, F32),
                   jax.ShapeDtypeStruct(st_cconv.shape, F32)],
        compiler_params=_params(1, vmem),
        name="mixer_sample",
    )(u_s, st_pool, st_sconv, st_cconv, wl["w_pool"], row(wl["pool_scale"]), wl["sconv_w"],
      row(wl["sconv_gain"]), wl["cconv_w"], row(wl["cconv_b"]), row(wl["cconv_ln_g"]),
      row(wl["cconv_ln_b"]), row(wl["cconv_gain"]))


def _attn_prompt_kernel(q_ref, k_ref, v_ref, o_ref, *, scale):
    kb = k_ref[...].astype(BF16)
    vb = v_ref[...].astype(BF16)
    s = lax.dot_general(q_ref[...], kb, (((1,), (1,)), ((), ())),
                        preferred_element_type=F32) * scale
    e = jnp.exp(s - jnp.max(s, axis=-1, keepdims=True))
    p = e / jnp.sum(e, axis=-1, keepdims=True)
    o_ref[...] = jnp.dot(p.astype(BF16), vb, preferred_element_type=F32).astype(o_ref.dtype)


def attention_prompt(q, k, v, n_seq, seq_len, n_mem):
    d = q.shape[1]
    hd = d // X_HEADS
    tq = _pick_tile(seq_len, (1024, 512, 256, 128, 64))
    nq = seq_len // tq
    kernel = functools.partial(_attn_prompt_kernel, scale=float(hd) ** -0.5)
    vmem = 4 * tq * hd * 2 + 4 * n_mem * hd * 4 + 4 * n_mem * hd * 2 + 6 * tq * n_mem * 4 + tq * hd * 4
    return pl.pallas_call(
        kernel,
        grid=(n_seq, X_HEADS, nq),
        in_specs=[pl.BlockSpec((tq, hd), lambda b, h, i: (b * nq + i, h)),
                  pl.BlockSpec((n_mem, hd), lambda b, h, i: (b, h)),
                  pl.BlockSpec((n_mem, hd), lambda b, h, i: (b, h))],
        out_specs=pl.BlockSpec((tq, hd), lambda b, h, i: (b * nq + i, h)),
        out_shape=jax.ShapeDtypeStruct((n_seq * seq_len, d), BF16),
        compiler_params=_params(3, vmem),
        name="attn_prompt",
    )(q, k, v)


def _attn_sample_kernel(q_ref, k_ref, v_ref, o_ref, *, scale):
    q = q_ref[...]
    s = jnp.sum(k_ref[...] * q[None], axis=-1, keepdims=True) * scale
    e = jnp.exp(s - jnp.max(s, axis=0, keepdims=True))
    p = e / jnp.sum(e, axis=0, keepdims=True)
    o_ref[...] = jnp.sum(p * v_ref[...], axis=0)


def attention_sample(q_s, cache_k, cache_v, layer):
    n_seq, heads, hd = q_s.shape
    n_mem = cache_k.shape[2]
    kernel = functools.partial(_attn_sample_kernel, scale=float(hd) ** -0.5)
    blk = n_mem * 8 * hd * 4
    cache_spec = pl.BlockSpec((None, None, n_mem, heads, hd), lambda b: (layer, b, 0, 0, 0))
    return pl.pallas_call(
        kernel,
        grid=(n_seq,),
        in_specs=[pl.BlockSpec((None, heads, hd), lambda b: (b, 0, 0)), cache_spec, cache_spec],
        out_specs=pl.BlockSpec((None, heads, hd), lambda b: (b, 0, 0)),
        out_shape=jax.ShapeDtypeStruct((n_seq, heads, hd), F32),
        compiler_params=_params(1, 5 * blk),
        name="attn_sample",
    )(q_s, cache_k, cache_v)


def _router_kernel(x_ref, g_ref, w_ref, idx_ref, gate_ref):
    h = _rms_rows(x_ref[...], g_ref[...])
    logits = jnp.dot(h, w_ref[...], preferred_element_type=F32, precision=lax.Precision.HIGHEST)
    n_e = logits.shape[-1]
    iota = lax.broadcasted_iota(jnp.int32, logits.shape, 1)
    m1 = jnp.max(logits, axis=-1, keepdims=True)
    i1 = jnp.min(jnp.where(logits == m1, iota, n_e), axis=-1, keepdims=True)
    rest = jnp.where(iota == i1, -jnp.inf, logits)
    m2 = jnp.max(rest, axis=-1, keepdims=True)
    i2 = jnp.min(jnp.where(rest == m2, iota, n_e), axis=-1, keepdims=True)
    e2 = jnp.exp(m2 - m1)
    denom = 1.0 + e2
    idx_ref[...] = jnp.concatenate([i1, i2], axis=-1)
    gate_ref[...] = jnp.concatenate([1.0 / denom, e2 / denom], axis=-1)


def router(x, gain, router_w, layer):
    t, d = x.shape
    n_e = router_w.shape[-1]
    bt = _pick_tile(t, (416, 512, 256, 128, 64, 8))
    return pl.pallas_call(
        _router_kernel,
        grid=(t // bt,),
        in_specs=[pl.BlockSpec((bt, d), lambda i: (i, 0)),
                  pl.BlockSpec((1, d), lambda i: (0, 0)),
                  pl.BlockSpec((None, d, n_e), lambda i: (layer, 0, 0))],
        out_specs=[pl.BlockSpec((bt, TOP_K), lambda i: (i, 0)),
                   pl.BlockSpec((bt, TOP_K), lambda i: (i, 0))],
        out_shape=[jax.ShapeDtypeStruct((t, TOP_K), jnp.int32),
                   jax.ShapeDtypeStruct((t, TOP_K), F32)],
        compiler_params=_params(1, 4 * bt * d * 4 + 2 * d * 128 * 4),
        name="router",
    )(x, gain.reshape(1, d), router_w)


def _gather_norm_kernel(tok_ref, used_ref, x_hbm, g_ref, o_ref, buf_ref, sem, *, bm):
    i = pl.program_id(0)
    used = used_ref[0]

    def row_copy(tile, j, slot):
        tok = tok_ref[tile * bm + j]
        return pltpu.make_async_copy(x_hbm.at[pl.ds(tok, 1)], buf_ref.at[slot, pl.ds(j, 1)],
                                     sem.at[slot])

    def request(tile, slot):
        def start(j, c):
            row_copy(tile, j, slot).start()
            return c

        lax.fori_loop(0, bm, start, 0, unroll=8)

    @pl.when(jnp.logical_and(i == 0, used > 0))
    def _():
        request(0, 0)

    @pl.when(i + 1 < used)
    def _():
        request(i + 1, lax.rem(i + 1, 2))

    @pl.when(i < used)
    def _():
        slot = lax.rem(i, 2)

        def wait(j, c):
            row_copy(i, j, slot).wait()
            return c

        lax.fori_loop(0, bm, wait, 0, unroll=8)
        o_ref[...] = _rms_rows(buf_ref[slot], g_ref[...]).astype(o_ref.dtype)

    @pl.when(i >= used)
    def _():
        o_ref[...] = jnp.zeros_like(o_ref)


def gather_norm(x, gain, row_token, used_tiles, bm, n_tiles):
    t, d = x.shape
    kernel = functools.partial(_gather_norm_kernel, bm=bm)
    grid_spec = pltpu.PrefetchScalarGridSpec(
        num_scalar_prefetch=2,
        grid=(n_tiles,),
        in_specs=[pl.BlockSpec(memory_space=pl.ANY),
                  pl.BlockSpec((1, d), lambda i, *_: (0, 0))],
        out_specs=pl.BlockSpec((bm, d), lambda i, *_: (i, 0)),
        scratch_shapes=[pltpu.VMEM((2, bm, d), F32), pltpu.SemaphoreType.DMA((2,))],
    )
    return pl.pallas_call(
        kernel,
        grid_spec=grid_spec,
        out_shape=jax.ShapeDtypeStruct((n_tiles * bm, d), BF16),
        compiler_params=_params(1, bm * d * 4 * 4 + 2 * bm * d * 2),
        name="moe_gather",
    )(row_token, used_tiles, x, gain.reshape(1, d))


def _combine_kernel(pos_ref, x_ref, gate_ref, y_hbm, o_ref, buf_ref, sem, *, bt):
    i = pl.program_id(0)

    def row_copy(j, k):
        p = pos_ref[(i * bt + j) * TOP_K + k]
        return pltpu.make_async_copy(y_hbm.at[pl.ds(p, 1)], buf_ref.at[k, pl.ds(j, 1)], sem)

    def start(j, c):
        for k in range(TOP_K):
            row_copy(j, k).start()
        return c

    def wait(j, c):
        for k in range(TOP_K):
            row_copy(j, k).wait()
        return c

    lax.fori_loop(0, bt, start, 0, unroll=8)
    lax.fori_loop(0, bt, wait, 0, unroll=8)
    out = x_ref[...]
    gates = gate_ref[...]
    for k in range(TOP_K):
        out = out + gates[:, k:k + 1] * buf_ref[k]
    o_ref[...] = out


def moe_combine(x, gates, y, pos_flat):
    t, d = x.shape
    bt = _pick_tile(t, (416, 256, 128, 64, 8))
    kernel = functools.partial(_combine_kernel, bt=bt)
    grid_spec = pltpu.PrefetchScalarGridSpec(
        num_scalar_prefetch=1,
        grid=(t // bt,),
        in_specs=[pl.BlockSpec((bt, d), lambda i, *_: (i, 0)),
                  pl.BlockSpec((bt, TOP_K), lambda i, *_: (i, 0)),
                  pl.BlockSpec(memory_space=pl.ANY)],
        out_specs=pl.BlockSpec((bt, d), lambda i, *_: (i, 0)),
        scratch_shapes=[pltpu.VMEM((TOP_K, bt, d), F32), pltpu.SemaphoreType.DMA],
    )
    return pl.pallas_call(
        kernel,
        grid_spec=grid_spec,
        out_shape=jax.ShapeDtypeStruct((t, d), F32),
        compiler_params=_params(1, bt * d * 4 * (TOP_K + 5)),
        name="moe_combine",
    )(pos_flat, x, gates, y)


def _routing_tables(idx, bm, n_tiles):
    t = idx.shape[0]
    e_flat = idx.reshape(-1)
    onehot = (e_flat[:, None] == jnp.arange(N_EXPERTS, dtype=jnp.int32)[None, :]).astype(jnp.int32)
    rank = jnp.sum((jnp.cumsum(onehot, axis=0) - onehot) * onehot, axis=1)
    counts = jnp.sum(onehot, axis=0)
    tiles_e = (counts + bm - 1) // bm
    group_starts = jnp.concatenate([jnp.zeros((1,), jnp.int32),
                                    jnp.cumsum(tiles_e).astype(jnp.int32)])
    pos = (group_starts[:-1] * bm)[e_flat] + rank
    row_token = jnp.zeros((n_tiles * bm,), jnp.int32).at[pos].set(
        jnp.arange(t * TOP_K, dtype=jnp.int32) // TOP_K)
    return pos.astype(jnp.int32), row_token, group_starts


def _tiles(t):
    return dict(
        bm=_pick_tile(t, (1040, 832, 640, 512, 416, 256, 128, 64, 8)),
        bm_tall=_pick_tile(t, (1664, 832, 640, 512, 416, 256, 128, 64, 8)),
        bm_long_k=_pick_tile(t, (640, 416, 320, 256, 128, 64, 8)),
        bn=512,
        bn_pair=256,
    )


def kernel(x_prompt, x_sample, cache_mem_k, cache_mem_v, state_pool, state_sconv, state_cconv,
           mem_prompt, norm_mix, w_in, w_pool, pool_scale, sconv_w, sconv_gain, cconv_w, cconv_b,
           cconv_ln_g, cconv_ln_b, cconv_gain, w_out, norm_x, mem_norm, w_q, w_k, w_v, w_o,
           norm_ffn, ffn_w_gate, ffn_w_up, ffn_w_down, router_w, moe_w_gate, moe_w_up,
           moe_w_down, final_norm):
    n_seq, seq_len, d = x_prompt.shape
    n_dec = x_sample.shape[0]
    depth = w_in.shape[0]
    n_mem = mem_prompt.shape[1]
    hd = d // X_HEADS
    pool_w = pool_scale.shape[1]
    sconv_ch = sconv_gain.shape[1]
    cconv_ch = cconv_gain.shape[1]
    dims = (pool_w, sconv_ch, cconv_ch)
    tp = n_seq * seq_len
    t = tp + n_dec
    tiles = _tiles(t)
    bm, bn = tiles["bm"], tiles["bn"]

    x = jnp.concatenate([x_prompt.reshape(tp, d), x_sample.reshape(n_dec, d)], axis=0)
    mem = mem_prompt.reshape(n_seq * n_mem, d)
    bm_mem = _pick_tile(n_seq * n_mem, (512, 256, 128, 64, 8))

    mem_k, mem_v = [], []
    pool_p, pool_s, sc_p, sc_s, cc_p, cc_s = [], [], [], [], [], []
    for i in range(depth):
        wl = dict(w_pool=w_pool[i], pool_scale=pool_scale[i], sconv_w=sconv_w[i],
                  sconv_gain=sconv_gain[i], cconv_w=cconv_w[i], cconv_b=cconv_b[i],
                  cconv_ln_g=cconv_ln_g[i], cconv_ln_b=cconv_ln_b[i], cconv_gain=cconv_gain[i])

        h = rmsnorm(x, norm_mix[i], BF16)
        u = matmul(h, [w_in], (i,), out_dtype=F32, bm=bm, bn=bn, name="w_in")
        o_p, stb_p, stc_p = mixer_prompt(u, n_seq, seq_len, wl, dims)
        o_s, stp_s, stb_s, stc_s = mixer_sample(
            u[tp:], state_pool[i].transpose(1, 0, 2), state_sconv[i].transpose(1, 0, 2),
            state_cconv[i].transpose(1, 0, 2), wl, dims)
        o = jnp.concatenate([o_p, o_s], axis=0)
        x = matmul(o, [w_out], (i,), out_dtype=F32, bm=bm, bn=bn, residual=x, name="w_out")
        pool_buf = state_pool.shape[2]
        pool_p.append(u[:tp, :pool_w].reshape(n_seq, seq_len, pool_w)[:, seq_len - pool_buf:])
        pool_s.append(stp_s.transpose(1, 0, 2))
        sc_p.append(stb_p)
        sc_s.append(stb_s.transpose(1, 0, 2))
        cc_p.append(stc_p)
        cc_s.append(stc_s.transpose(1, 0, 2))

        mn = rmsnorm(mem, mem_norm[i], BF16)
        k_p = matmul(mn, [w_k], (i,), out_dtype=F32, bm=bm_mem, bn=bn, name="w_k")
        v_p = matmul(mn, [w_v], (i,), out_dtype=F32, bm=bm_mem, bn=bn, name="w_v")
        mem_k.append(k_p.reshape(n_seq, n_mem, X_HEADS, hd))
        mem_v.append(v_p.reshape(n_seq, n_mem, X_HEADS, hd))
        hq = rmsnorm(x, norm_x[i], BF16)
        q = matmul(hq, [w_q], (i,), out_dtype=BF16, bm=tiles["bm_tall"], bn=bn, name="w_q")
        a_p = attention_prompt(q, k_p, v_p, n_seq, seq_len, n_mem)
        a_s = attention_sample(q[tp:].astype(F32).reshape(n_dec, X_HEADS, hd),
                               cache_mem_k, cache_mem_v, i)
        a = jnp.concatenate([a_p, a_s.reshape(n_dec, d).astype(BF16)], axis=0)
        x = matmul(a, [w_o], (i,), out_dtype=F32, bm=bm, bn=bn, residual=x, name="w_o")

        j = i // 2
        if i % 2 == 0:
            hf = rmsnorm(x, norm_ffn[i], BF16)
            mid = matmul(hf, [ffn_w_gate, ffn_w_up], (j,), out_dtype=BF16, bm=tiles["bm_tall"],
                         bn=tiles["bn_pair"], epilogue="swiglu", name="ffn_gate_up")
            for kb in range(2):
                x = matmul(mid, [ffn_w_down], (j,), out_dtype=F32, bm=tiles["bm_long_k"], bn=bn,
                           residual=x, k_window=(kb, 2), name="ffn_down")
        else:
            bmg = MOE_ROW_TILE
            n_tiles = -(-(t * TOP_K) // bmg) + N_EXPERTS
            idx, gates = router(x, norm_ffn[i], router_w, j)
            pos, row_token, group_starts = _routing_tables(idx, bmg, n_tiles)
            xs = gather_norm(x, norm_ffn[i], row_token, group_starts[N_EXPERTS:], bmg, n_tiles)
            mid = matmul(xs, [moe_w_gate, moe_w_up], (j,), out_dtype=BF16, bm=bmg, bn=bn,
                         epilogue="swiglu", group_starts=group_starts, name="moe_gate_up")
            y = matmul(mid, [moe_w_down], (j,), out_dtype=F32, bm=bmg, bn=bn,
                       group_starts=group_starts, name="moe_down")
            x = moe_combine(x, gates, y, pos)

    y_all = rmsnorm(x, final_norm, F32)
    y_prompt = y_all[:tp].reshape(n_seq, seq_len, d)
    y_sample = y_all[tp:].reshape(n_dec, 1, d)
    return (y_prompt, y_sample, jnp.stack(mem_k), jnp.stack(mem_v), jnp.stack(pool_p),
            jnp.stack(pool_s), jnp.stack(sc_p), jnp.stack(sc_s), jnp.stack(cc_p), jnp.stack(cc_s))
```
